```python
import jax, jax.numpy as jnp
from jax import lax
import numpy as np

D_MODEL = 4096
BATCH = 1
SEQ = 16384
DEPTH = 2

CHUNK = 64
EPS = 1e-6

CONV_WIDTH = D_MODEL // 2
CONV_K = 3
SGU_WIDTH = D_MODEL // 2
SGU_HEADS = 16
SGU_BLOCK = 128
EV_IN = 3 * CONV_WIDTH + 2 * SGU_WIDTH
EV_MIX = CONV_WIDTH + SGU_WIDTH

SSM_INNER = 2 * D_MODEL
SSM_HEADDIM = 64
SSM_HEADS = SSM_INNER // SSM_HEADDIM
SSM_GROUPS = 8
SSM_STATE = 128
SSM_CONV_K = 4
SSM_CHUNK = CHUNK
SSM_CONV_DIM = SSM_INNER + 2 * SSM_GROUPS * SSM_STATE
OD_IN = SSM_INNER + SSM_CONV_DIM + SSM_HEADS

PEER_HEADS = 8
PEER_NKEYS = 128
PEER_EXPERTS = PEER_NKEYS * PEER_NKEYS
PEER_QDIM = 256
PEER_HALF = PEER_QDIM // 2
PEER_TOPK = 16
PEER_BLOCK = 128

N_EVEN = (DEPTH + 1) // 2
N_ODD = DEPTH // 2

kernel_name = "hybrid_conv_sgu_ssd_peer_trunk"


def rms_norm(x, g):
    xf = x.astype(jnp.float32)
    y = xf * lax.rsqrt(jnp.mean(xf * xf, axis=-1, keepdims=True) + EPS)
    return (y * g.astype(jnp.float32)).astype(x.dtype)


def layer_norm(x, g):
    xf = x.astype(jnp.float32)
    mu = jnp.mean(xf, axis=-1, keepdims=True)
    var = jnp.mean(jnp.square(xf - mu), axis=-1, keepdims=True)
    return ((xf - mu) * lax.rsqrt(var + EPS) * g.astype(jnp.float32)).astype(x.dtype)


def causal_dwconv(x, w):
    K = w.shape[0]
    S = x.shape[1]
    xp = jnp.pad(x, ((0, 0), (K - 1, 0), (0, 0)))
    y = xp[:, 0:S] * w[0]
    for k in range(1, K):
        y = y + xp[:, k:k + S] * w[k]
    return y


def conv_sgu_mixer(h, w_in, conv_w, sgu_norm, sgu_w, sgu_b, w_out):
    Bsz, S, _ = h.shape
    proj = h @ w_in
    gb, gc, xt, zu, zv = jnp.split(
        proj, [CONV_WIDTH, 2 * CONV_WIDTH, 3 * CONV_WIDTH, 3 * CONV_WIDTH + SGU_WIDTH], axis=-1)
    y_conv = gb * causal_dwconv(gc * xt, conv_w)
    u = jax.nn.gelu(zu)
    v = layer_norm(jax.nn.gelu(zv), sgu_norm)
    nb = S // SGU_BLOCK
    vb = v.reshape(Bsz, nb, SGU_BLOCK, SGU_HEADS, SGU_WIDTH // SGU_HEADS)
    tril = jnp.tril(jnp.ones((SGU_BLOCK, SGU_BLOCK), dtype=bool))
    ws = jnp.where(tril[None], sgu_w, jnp.zeros_like(sgu_w))
    sv = jnp.einsum('gqp,bnpgc->bnqgc', ws, vb) + jnp.transpose(sgu_b)[:, :, None]
    y_sgu = u * sv.reshape(Bsz, S, SGU_WIDTH)
    return jnp.concatenate([y_conv, y_sgu], axis=-1) @ w_out


def ssd_scan(x, dt, A, Bm, Cm):
    Bsz, S, H, P = x.shape
    G, N = SSM_GROUPS, SSM_STATE
    R = H // G
    L = SSM_CHUNK
    nc = S // L
    xdt = x.astype(jnp.float32) * dt[..., None]
    dA = dt * A

    def to_chunks(t):
        return jnp.moveaxis(t.reshape(Bsz, nc, L, *t.shape[2:]), 1, 0)

    xs = to_chunks(xdt.reshape(Bsz, S, G, R, P))
    As = to_chunks(dA.reshape(Bsz, S, G, R))
    Bs = to_chunks(Bm.astype(jnp.float32))
    Cs = to_chunks(Cm.astype(jnp.float32))
    causal = jnp.tril(jnp.ones((L, L), dtype=bool))[None, :, :, None, None]

    def step(state, inp):
        xc, ac, bc, cc = inp
        acs = jnp.cumsum(ac, axis=1)
        seg = acs[:, :, None] - acs[:, None, :]
        decay = jnp.exp(jnp.where(causal, seg, -jnp.inf))
        cb = jnp.einsum('blgn,bsgn->blsg', cc, bc)
        y_diag = jnp.einsum('blsg,blsgr,bsgrp->blgrp', cb, decay, xc)
        y_off = jnp.einsum('blgn,bgrpn,blgr->blgrp', cc, state, jnp.exp(acs))
        to_end = jnp.exp(acs[:, -1:] - acs)
        new_state = state * jnp.exp(acs[:, -1])[..., None, None] + jnp.einsum(
            'blgn,blgr,blgrp->bgrpn', bc, to_end, xc)
        return new_state, y_diag + y_off

    init = jnp.zeros((Bsz, G, R, P, N), jnp.float32)
    _, ys = lax.scan(step, init, (xs, As, Bs, Cs))
    return jnp.moveaxis(ys, 0, 1).reshape(Bsz, S, H, P)


def gated_group_rms(y, z, g):
    yz = (y * jax.nn.silu(z)).astype(jnp.float32)
    shp = yz.shape
    yg = yz.reshape(*shp[:-1], SSM_GROUPS, shp[-1] // SSM_GROUPS)
    yg = yg * lax.rsqrt(jnp.mean(yg * yg, axis=-1, keepdims=True) + EPS)
    return (yg.reshape(shp) * g.astype(jnp.float32)).astype(y.dtype)


def mamba2_mixer(h, w_in, conv_w, conv_b, dt_bias, a_log, d_skip, norm_g, w_out):
    Bsz, S, _ = h.shape
    proj = h @ w_in
    z, xbc, dt = jnp.split(proj, [SSM_INNER, SSM_INNER + SSM_CONV_DIM], axis=-1)
    xbc = jax.nn.silu(causal_dwconv(xbc, conv_w) + conv_b)
    xs, Bm, Cm = jnp.split(xbc, [SSM_INNER, SSM_INNER + SSM_GROUPS * SSM_STATE], axis=-1)
    dt = jax.nn.softplus(dt.astype(jnp.float32) + dt_bias.astype(jnp.float32))
    A = -jnp.exp(a_log.astype(jnp.float32))
    xh = xs.reshape(Bsz, S, SSM_HEADS, SSM_HEADDIM)
    y = ssd_scan(xh, dt, A,
                 Bm.reshape(Bsz, S, SSM_GROUPS, SSM_STATE),
                 Cm.reshape(Bsz, S, SSM_GROUPS, SSM_STATE))
    y = y + d_skip.astype(jnp.float32)[:, None] * xh.astype(jnp.float32)
    y = y.reshape(Bsz, S, SSM_INNER).astype(h.dtype)
    return gated_group_rms(y, z, norm_g) @ w_out


def peer_ffn(h, wq, keys, u_tab, v_tab):
    Bsz, S, D = h.shape
    T = Bsz * S
    H, K = PEER_HEADS, PEER_TOPK
    xt = h.reshape(T, D)
    q = (xt @ wq).reshape(T, H, 2, PEER_HALF)
    s = jnp.einsum('thic,hikc->thik', q, keys).astype(jnp.float32)
    s_top, i_top = lax.top_k(s, K)
    cand = s_top[:, :, 0, :, None] + s_top[:, :, 1, None, :]
    cand_idx = i_top[:, :, 0, :, None] * PEER_NKEYS + i_top[:, :, 1, None, :]
    best, pos = lax.top_k(cand.reshape(T, H, K * K), K)
    expert = jnp.take_along_axis(cand_idx.reshape(T, H, K * K), pos, axis=-1)
    gate = jax.nn.softmax(best, axis=-1).astype(h.dtype)
    nb = T // PEER_BLOCK

    def block(args):
        xb, eb, gb = args
        act = jax.nn.gelu(jnp.einsum('td,thkd->thk', xb, u_tab[eb]))
        return jnp.einsum('thk,thkd->td', gb * act, v_tab[eb])

    out = lax.map(block, (xt.reshape(nb, PEER_BLOCK, D),
                          expert.reshape(nb, PEER_BLOCK, H, K),
                          gate.reshape(nb, PEER_BLOCK, H, K)))
    return out.reshape(Bsz, S, D)


def setup_inputs(seed: int = 0) -> dict:
    key = jax.random.key(seed)
    ks = jax.random.split(key, 24)
    f32 = jnp.float32

    def nrm(k, shape, scale):
        return jax.random.normal(k, shape, f32) * scale

    dt0 = jnp.exp(jax.random.uniform(ks[11], (N_ODD, SSM_HEADS), f32,
                                     np.log(1e-3).astype(np.float32), np.log(1e-1).astype(np.float32)))
    return {
        'x': nrm(ks[0], (BATCH, SEQ, D_MODEL), 1.0),
        'mix_norm': 1.0 + nrm(ks[1], (DEPTH, D_MODEL), 0.02),
        'ffn_norm': 1.0 + nrm(ks[2], (DEPTH, D_MODEL), 0.02),
        'final_norm': 1.0 + nrm(ks[3], (D_MODEL,), 0.02),
        'ev_w_in': nrm(ks[4], (N_EVEN, D_MODEL, EV_IN), D_MODEL ** -0.5),
        'ev_conv_w': nrm(ks[5], (N_EVEN, CONV_K, CONV_WIDTH), CONV_K ** -0.5),
        'ev_sgu_norm': 1.0 + nrm(ks[6], (N_EVEN, SGU_WIDTH), 0.02),
        'ev_sgu_w': nrm(ks[7], (N_EVEN, SGU_HEADS, SGU_BLOCK, SGU_BLOCK), SGU_BLOCK ** -0.5),
        'ev_sgu_b': 1.0 + nrm(ks[8], (N_EVEN, SGU_HEADS, SGU_BLOCK), 0.1),
        'ev_w_out': nrm(ks[9], (N_EVEN, EV_MIX, D_MODEL), EV_MIX ** -0.5),
        'od_w_in': nrm(ks[10], (N_ODD, D_MODEL, OD_IN), D_MODEL ** -0.5),
        'od_conv_w': nrm(ks[12], (N_ODD, SSM_CONV_K, SSM_CONV_DIM), SSM_CONV_K ** -0.5),
        'od_conv_b': nrm(ks[13], (N_ODD, SSM_CONV_DIM), 0.02),
        'od_dt_bias': dt0 + jnp.log(-jnp.expm1(-dt0)),
        'od_a_log': jnp.log(jax.random.uniform(ks[14], (N_ODD, SSM_HEADS), f32, 1.0, 16.0)),
        'od_d': 1.0 + nrm(ks[15], (N_ODD, SSM_HEADS), 0.1),
        'od_norm': 1.0 + nrm(ks[16], (N_ODD, SSM_INNER), 0.02),
        'od_w_out': nrm(ks[17], (N_ODD, SSM_INNER, D_MODEL), SSM_INNER ** -0.5),
        'peer_wq': nrm(ks[18], (DEPTH, D_MODEL, PEER_HEADS * PEER_QDIM), D_MODEL ** -0.5),
        'peer_keys': nrm(ks[19], (DEPTH, PEER_HEADS, 2, PEER_NKEYS, PEER_HALF), PEER_HALF ** -0.5),
        'peer_u': nrm(ks[20], (DEPTH, PEER_EXPERTS, D_MODEL), D_MODEL ** -0.5),
        'peer_v': nrm(ks[21], (DEPTH, PEER_EXPERTS, D_MODEL), PEER_HEADS ** -0.5),
    }


def reference(x, mix_norm, ffn_norm, final_norm,
              ev_w_in, ev_conv_w, ev_sgu_norm, ev_sgu_w, ev_sgu_b, ev_w_out,
              od_w_in, od_conv_w, od_conv_b, od_dt_bias, od_a_log, od_d, od_norm, od_w_out,
              peer_wq, peer_keys, peer_u, peer_v):
    h = x
    for i in range(DEPTH):
        hn = rms_norm(h, mix_norm[i])
        j = i // 2
        if i % 2 == 0:
            mix = conv_sgu_mixer(hn, ev_w_in[j], ev_conv_w[j], ev_sgu_norm[j],
                                 ev_sgu_w[j], ev_sgu_b[j], ev_w_out[j])
        else:
            mix = mamba2_mixer(hn, od_w_in[j], od_conv_w[j], od_conv_b[j], od_dt_bias[j],
                               od_a_log[j], od_d[j], od_norm[j], od_w_out[j])
        h = h + mix
        h = h + peer_ffn(rms_norm(h, ffn_norm[i]), peer_wq[i], peer_keys[i], peer_u[i], peer_v[i])
    return rms_norm(h, final_norm)
```

```python
import functools

import jax
import jax.numpy as jnp
from jax import lax
from jax.experimental import pallas as pl
from jax.experimental.pallas import tpu as pltpu

F32 = jnp.float32
BF16 = jnp.bfloat16

EPS = 1e-6
LANES = 128
VMEM_LIMIT = 56 * 1024 * 1024

D_MODEL = 4096
CONV_WIDTH = 2048
CONV_K = 3
SGU_WIDTH = 2048
SGU_HEADS = 16
SGU_BLOCK = 128
SSM_INNER = 8192
SSM_HEADDIM = 64
SSM_HEADS = 128
SSM_GROUPS = 8
SSM_STATE = 128
SSM_CONV_K = 4
SSM_CONV_DIM = SSM_INNER + 2 * SSM_GROUPS * SSM_STATE
SSD_CHUNK = 128
PEER_HEADS = 8
PEER_NKEYS = 128
PEER_HALF = 128
PEER_TOPK = 16
PEER_CAND = [(a, b) for a in range(PEER_TOPK) for b in range(PEER_TOPK)
             if (a + 1) * (b + 1) <= PEER_TOPK]
PEER_NCAND = len(PEER_CAND)
PEER_NCAND_PAD = -(-PEER_NCAND // 8) * 8


def _params(sem):
    return pltpu.CompilerParams(dimension_semantics=sem, vmem_limit_bytes=VMEM_LIMIT)


def _mm_kernel(*refs, nk, has_res):
    if has_res:
        a_ref, b_ref, r_ref, o_ref, *scratch = refs
    else:
        a_ref, b_ref, o_ref, *scratch = refs
        r_ref = None

    def finish(acc):
        if has_res:
            acc = acc + r_ref[...]
        o_ref[...] = acc.astype(o_ref.dtype)

    part = jnp.dot(a_ref[...], b_ref[...], preferred_element_type=F32)
    if nk == 1:
        finish(part)
    else:
        acc_ref, = scratch
        k = pl.program_id(2)

        @pl.when(k == 0)
        def _():
            acc_ref[...] = part

        @pl.when(k > 0)
        def _():
            acc_ref[...] += part

        @pl.when(k == nk - 1)
        def _():
            finish(acc_ref[...])


def matmul(a, b, *, bm, bn, bk=None, out_dtype=F32, res=None, name="mm"):
    m, kdim = a.shape
    _, n = b.shape
    bk = kdim if bk is None else bk
    bm, bn = min(bm, m), min(bn, n)
    nk = kdim // bk
    assert m % bm == 0 and n % bn == 0 and kdim % bk == 0
    in_specs = [pl.BlockSpec((bm, bk), lambda i, j, k: (i, k)),
                pl.BlockSpec((bk, bn), lambda i, j, k: (k, j))]
    args = [a, b]
    if res is not None:
        in_specs.append(pl.BlockSpec((bm, bn), lambda i, j, k: (i, j)))
        args.append(res)
    scratch = [pltpu.VMEM((bm, bn), F32)] if nk > 1 else []
    return pl.pallas_call(
        functools.partial(_mm_kernel, nk=nk, has_res=res is not None),
        out_shape=jax.ShapeDtypeStruct((m, n), out_dtype),
        grid=(m // bm, n // bn, nk),
        in_specs=in_specs,
        out_specs=pl.BlockSpec((bm, bn), lambda i, j, k: (i, j)),
        scratch_shapes=scratch,
        compiler_params=_params(("parallel", "parallel", "arbitrary")),
        name=name,
    )(*args)


def _rms(x, g):
    return x * lax.rsqrt(jnp.mean(x * x, axis=-1, keepdims=True) + EPS) * g


def _rmsnorm_kernel(x_ref, g_ref, o_ref, *, transpose_out):
    y = _rms(x_ref[...], g_ref[...])
    if transpose_out:
        y = y.T
    o_ref[...] = y.astype(o_ref.dtype)


def rmsnorm(x, g, *, transpose_out, bt=256):
    t, d = x.shape
    if transpose_out:
        out_shape, out_spec = (d, t), pl.BlockSpec((d, bt), lambda i: (0, i))
    else:
        out_shape, out_spec = (t, d), pl.BlockSpec((bt, d), lambda i: (i, 0))
    return pl.pallas_call(
        functools.partial(_rmsnorm_kernel, transpose_out=transpose_out),
        out_shape=jax.ShapeDtypeStruct(out_shape, BF16),
        grid=(t // bt,),
        in_specs=[pl.BlockSpec((bt, d), lambda i: (i, 0)),
                  pl.BlockSpec((1, d), lambda i: (0, 0))],
        out_specs=out_spec,
        compiler_params=_params(("parallel",)),
        name="rmsnorm_t" if transpose_out else "rmsnorm",
    )(x, g.reshape(1, d))


def _resid_norm_kernel(h_ref, dt_ref, g_ref, *o_refs, want_h):
    h = h_ref[...] + dt_ref[...].T
    if want_h:
        hn_ref, n_ref = o_refs
        hn_ref[...] = h
    else:
        n_ref, = o_refs
    n_ref[...] = _rms(h, g_ref[...]).astype(n_ref.dtype)


def resid_norm(h, delta_t, g, *, want_h, norm_dtype, bt=256):
    t, d = h.shape
    row = pl.BlockSpec((bt, d), lambda i: (i, 0))
    norm_shape = jax.ShapeDtypeStruct((t, d), norm_dtype)
    if want_h:
        out_shape, out_specs = (jax.ShapeDtypeStruct((t, d), F32), norm_shape), (row, row)
    else:
        out_shape, out_specs = norm_shape, row
    return pl.pallas_call(
        functools.partial(_resid_norm_kernel, want_h=want_h),
        out_shape=out_shape,
        grid=(t // bt,),
        in_specs=[row, pl.BlockSpec((d, bt), lambda i: (0, i)),
                  pl.BlockSpec((1, d), lambda i: (0, 0))],
        out_specs=out_specs,
        compiler_params=_params(("parallel",)),
        name="resid_norm",
    )(h, delta_t, g.reshape(1, d))


def _shift_rows(p, prev, k):
    rolled = pltpu.roll(p, k, axis=0)
    head = pltpu.roll(prev, k, axis=0)
    rows = lax.broadcasted_iota(jnp.int32, (8, p.shape[1]), 0)
    fixed = jnp.where(rows < k, head, rolled[:8])
    return jnp.concatenate([fixed, rolled[8:]], axis=0)


def _evmix_kernel(proj_ref, cw_ref, ng_ref, sw_ref, sbt_ref, o_ref, carry_ref):
    @pl.when(pl.program_id(0) == 0)
    def _():
        carry_ref[...] = jnp.zeros_like(carry_ref)

    c = CONV_WIDTH
    gb = proj_ref[:, 0:c]
    p = proj_ref[:, c:2 * c] * proj_ref[:, 2 * c:3 * c]
    prev = carry_ref[...]
    conv = p * cw_ref[CONV_K - 1:CONV_K, :]
    for k in range(1, CONV_K):
        conv = conv + _shift_rows(p, prev, k) * cw_ref[CONV_K - 1 - k:CONV_K - k, :]
    carry_ref[...] = p[p.shape[0] - 8:]
    o_ref[:, 0:c] = (gb * conv).astype(o_ref.dtype)

    u = jax.nn.gelu(proj_ref[:, 3 * c:3 * c + SGU_WIDTH])
    v = jax.nn.gelu(proj_ref[:, 3 * c + SGU_WIDTH:3 * c + 2 * SGU_WIDTH])
    mu = jnp.mean(v, axis=-1, keepdims=True)
    vc = v - mu
    var = jnp.mean(vc * vc, axis=-1, keepdims=True)
    vn = (vc * lax.rsqrt(var + EPS) * ng_ref[...]).astype(BF16)
    q_idx = lax.broadcasted_iota(jnp.int32, (SGU_BLOCK, SGU_BLOCK), 0)
    p_idx = lax.broadcasted_iota(jnp.int32, (SGU_BLOCK, SGU_BLOCK), 1)
    hw = SGU_WIDTH // SGU_HEADS
    for g in range(SGU_HEADS):
        ws = jnp.where(q_idx >= p_idx, sw_ref[g], 0.0).astype(BF16)
        sv = jnp.dot(ws, vn[:, g * hw:(g + 1) * hw], preferred_element_type=F32)
        sv = sv + sbt_ref[:, g:g + 1]
        o_ref[:, c + g * hw:c + (g + 1) * hw] = (u[:, g * hw:(g + 1) * hw] * sv).astype(o_ref.dtype)


def evmix(proj, conv_w, sgu_norm, sgu_w, sgu_b):
    t, width = proj.shape
    bt = SGU_BLOCK
    return pl.pallas_call(
        _evmix_kernel,
        out_shape=jax.ShapeDtypeStruct((t, CONV_WIDTH + SGU_WIDTH), BF16),
        grid=(t // bt,),
        in_specs=[pl.BlockSpec((bt, width), lambda i: (i, 0)),
                  pl.BlockSpec((CONV_K, CONV_WIDTH), lambda i: (0, 0)),
                  pl.BlockSpec((1, SGU_WIDTH), lambda i: (0, 0)),
                  pl.BlockSpec((SGU_HEADS, SGU_BLOCK, SGU_BLOCK), lambda i: (0, 0, 0)),
                  pl.BlockSpec((SGU_BLOCK, SGU_HEADS), lambda i: (0, 0))],
        out_specs=pl.BlockSpec((bt, CONV_WIDTH + SGU_WIDTH), lambda i: (i, 0)),
        scratch_shapes=[pltpu.VMEM((8, CONV_WIDTH), F32)],
        compiler_params=_params(("arbitrary",)),
        name="evmix",
    )(proj, conv_w, sgu_norm.reshape(1, SGU_WIDTH), sgu_w, jnp.transpose(sgu_b))


def _top16(s):
    n, width = s.shape
    rows = lax.broadcasted_iota(jnp.int32, s.shape, 0).astype(F32)
    slot = lax.broadcasted_iota(jnp.int32, (PEER_TOPK, width), 0)
    cur = s
    rank = jnp.full(s.shape, float(PEER_TOPK), F32)
    vals = jnp.zeros((PEER_TOPK, width), F32)
    for r in range(PEER_TOPK):
        m = jnp.max(cur, axis=0, keepdims=True)
        first = jnp.min(jnp.where(cur == m, rows, float(n)), axis=0, keepdims=True)
        sel = rows == first
        rank = jnp.where(sel, float(r), rank)
        cur = jnp.where(sel, -jnp.inf, cur)
        vals = jnp.where(slot == r, m, vals)
    return rank, vals


def _route_kernel(qt_ref, keys_ref, seta_ref, setb_ref, rowsel_ref,
                  lrow_ref, e1_ref, rank2_ref, e2_ref):
    def head(h, carry):
        base = pl.multiple_of(h * (2 * PEER_HALF), 2 * PEER_HALF)
        q1 = qt_ref[pl.ds(base, PEER_HALF), :].astype(BF16)
        q2 = qt_ref[pl.ds(base + PEER_HALF, PEER_HALF), :].astype(BF16)
        s1 = jnp.dot(keys_ref[h, 0], q1, preferred_element_type=F32)
        s2 = jnp.dot(keys_ref[h, 1], q2, preferred_element_type=F32)
        rank1, v1 = _top16(s1)
        rank2, v2 = _top16(s2)
        cand = (jnp.dot(seta_ref[...], v1, preferred_element_type=F32, precision=lax.Precision.HIGHEST)
                + jnp.dot(setb_ref[...], v2, preferred_element_type=F32, precision=lax.Precision.HIGHEST))
        crow = lax.broadcasted_iota(jnp.int32, cand.shape, 0)
        cand = jnp.where(crow < PEER_NCAND, cand, -jnp.inf)
        crank, cvals = _top16(cand)
        chosen = crank < float(PEER_TOPK)
        best = cvals[0:1, :]
        z = jnp.sum(jnp.where(chosen, jnp.exp(cand - best), 0.0), axis=0, keepdims=True)
        length = jnp.dot(rowsel_ref[...], chosen.astype(F32), preferred_element_type=F32)
        lrow = jnp.zeros_like(s1)
        for a in range(PEER_TOPK):
            lrow = jnp.where(rank1 == float(a), length[a:a + 1, :], lrow)
        lrow_ref[h] = lrow
        e1_ref[h] = jnp.exp(s1 - v1[0:1, :])
        rank2_ref[h] = rank2
        e2_ref[h] = jnp.exp(s2 - v2[0:1, :]) / z
        return carry

    lax.fori_loop(0, PEER_HEADS, head, 0)


def peer_route(qt, keys_bf16):
    _, t = qt.shape
    bt = LANES
    seta = jnp.zeros((PEER_NCAND_PAD, PEER_TOPK), F32)
    setb = jnp.zeros((PEER_NCAND_PAD, PEER_TOPK), F32)
    rowsel = jnp.zeros((PEER_TOPK, PEER_NCAND_PAD), F32)
    for k, (a, b) in enumerate(PEER_CAND):
        seta = seta.at[k, a].set(1.0)
        setb = setb.at[k, b].set(1.0)
        rowsel = rowsel.at[a, k].set(1.0)
    table = jax.ShapeDtypeStruct((PEER_HEADS, PEER_NKEYS, t), F32)
    tspec = pl.BlockSpec((PEER_HEADS, PEER_NKEYS, bt), lambda i: (0, 0, i))
    const2 = lambda i: (0, 0)
    return pl.pallas_call(
        _route_kernel,
        out_shape=(table, table, table, table),
        grid=(t // bt,),
        in_specs=[pl.BlockSpec((PEER_HEADS * 2 * PEER_HALF, bt), lambda i: (0, i)),
                  pl.BlockSpec((PEER_HEADS, 2, PEER_NKEYS, PEER_HALF), lambda i: (0, 0, 0, 0)),
                  pl.BlockSpec((PEER_NCAND_PAD, PEER_TOPK), const2),
                  pl.BlockSpec((PEER_NCAND_PAD, PEER_TOPK), const2),
                  pl.BlockSpec((PEER_TOPK, PEER_NCAND_PAD), const2)],
        out_specs=(tspec, tspec, tspec, tspec),
        compiler_params=_params(("parallel",)),
        name="peer_route",
    )(qt, keys_bf16, seta, setb, rowsel)


def _peer_act_kernel(u_ref, xt_ref, lrow_ref, e1_ref, rank2_ref, e2_ref, o_ref, *, nsub):
    e = pl.program_id(1)
    hid = jnp.dot(u_ref[...], xt_ref[...], preferred_element_type=F32)
    for ii in range(nsub):
        i = e * nsub + ii
        w = None
        for h in range(PEER_HEADS):
            length = lrow_ref[h, pl.ds(i, 1), :]
            gate1 = e1_ref[h, pl.ds(i, 1), :]
            contrib = jnp.where(rank2_ref[h] < length, e2_ref[h] * gate1, 0.0)
            w = contrib if w is None else w + contrib
        act = jax.nn.gelu(hid[ii * PEER_NKEYS:(ii + 1) * PEER_NKEYS])
        o_ref[ii * PEER_NKEYS:(ii + 1) * PEER_NKEYS, :] = (w * act).astype(o_ref.dtype)


def peer_act(u_bf16, xt, tables, *, bt=512, be=512):
    e, d = u_bf16.shape
    _, t = xt.shape
    tspec = pl.BlockSpec((PEER_HEADS, PEER_NKEYS, bt), lambda i, j: (0, 0, i))
    return pl.pallas_call(
        functools.partial(_peer_act_kernel, nsub=be // PEER_NKEYS),
        out_shape=jax.ShapeDtypeStruct((e, t), BF16),
        grid=(t // bt, e // be),
        in_specs=[pl.BlockSpec((be, d), lambda i, j: (j, 0)),
                  pl.BlockSpec((d, bt), lambda i, j: (0, i)),
                  tspec, tspec, tspec, tspec],
        out_specs=pl.BlockSpec((be, bt), lambda i, j: (j, i)),
        compiler_params=_params(("parallel", "arbitrary")),
        name="peer_act",
    )(u_bf16, xt, *tables)


def peer_ffn_t(xt, wq_t, keys_bf16, u_bf16, v_t):
    qt = matmul(wq_t, xt, bm=1024, bn=1024, name="peer_q")
    tables = peer_route(qt, keys_bf16)
    pt = peer_act(u_bf16, xt, tables)
    return matmul(v_t, pt, bm=1024, bn=1024, bk=2048, name="peer_out")


def _conv_silu_kernel(x_ref, w_ref, b_ref, o_ref, carry_ref):
    @pl.when(pl.program_id(1) == 0)
    def _():
        carry_ref[...] = jnp.zeros_like(carry_ref)

    x = x_ref[...]
    prev = carry_ref[...]
    acc = x * w_ref[SSM_CONV_K - 1:SSM_CONV_K, :] + b_ref[...]
    for k in range(1, SSM_CONV_K):
        acc = acc + _shift_rows(x, prev, k) * w_ref[SSM_CONV_K - 1 - k:SSM_CONV_K - k, :]
    carry_ref[...] = x[x.shape[0] - 8:]
    o_ref[...] = jax.nn.silu(acc).astype(o_ref.dtype)


def conv_silu(x, w, b, *, bt=512, bc=1024):
    t, c = x.shape
    return pl.pallas_call(
        _conv_silu_kernel,
        out_shape=jax.ShapeDtypeStruct((t, c), F32),
        grid=(c // bc, t // bt),
        in_specs=[pl.BlockSpec((bt, bc), lambda j, i: (i, j)),
                  pl.BlockSpec((SSM_CONV_K, bc), lambda j, i: (0, j)),
                  pl.BlockSpec((1, bc), lambda j, i: (0, j))],
        out_specs=pl.BlockSpec((bt, bc), lambda j, i: (i, j)),
        scratch_shapes=[pltpu.VMEM((8, bc), F32)],
        compiler_params=_params(("parallel", "arbitrary")),
        name="conv_silu",
    )(x, w, b.reshape(1, c))


def _ssd_kernel(x_ref, b_ref, c_ref, dt_ref, dtt_ref, bias_ref, biast_ref, alog_ref, alogt_ref,
                dskip_ref, y_ref, state_ref):
    L = SSD_CHUNK
    P = SSM_HEADDIM
    R = SSM_HEADS // SSM_GROUPS

    @pl.when(pl.program_id(1) == 0)
    def _():
        state_ref[...] = jnp.zeros_like(state_ref)

    dt = jax.nn.softplus(dt_ref[0] + bias_ref[0])
    dtt = jax.nn.softplus(dtt_ref[0] + biast_ref[0])
    da = dt * (-jnp.exp(alog_ref[0]))
    dat = dtt * (-jnp.exp(alogt_ref[0]))
    li = lax.broadcasted_iota(jnp.int32, (L, L), 0)
    si = lax.broadcasted_iota(jnp.int32, (L, L), 1)
    causal = li >= si
    acs = jnp.dot(causal.astype(F32), da, preferred_element_type=F32,
                  precision=lax.Precision.HIGHEST)
    acst = jnp.dot(dat, (li <= si).astype(F32), preferred_element_type=F32,
                   precision=lax.Precision.HIGHEST)

    bmat = b_ref[...]
    cmat = c_ref[...]
    cb = lax.dot_general(cmat.astype(BF16), bmat.astype(BF16), (((1,), (1,)), ((), ())),
                         preferred_element_type=F32)
    bt = bmat.T
    x = x_ref[...]
    for r in range(R):
        col = acs[:, r:r + 1]
        row = acst[r:r + 1, :]
        last = acst[r:r + 1, L - 1:L]
        decay = jnp.exp(jnp.where(causal, col - row, -jnp.inf))
        dtrow = dtt[r:r + 1, :]
        m_diag = (cb * decay * dtrow).astype(BF16)
        c_in = (cmat * jnp.exp(col)).astype(BF16)
        b_end = (bt * (jnp.exp(last - row) * dtrow)).astype(BF16)
        xr = x[:, r * P:(r + 1) * P]
        xr_b = xr.astype(BF16)
        st = state_ref[:, r * P:(r + 1) * P]
        y = (jnp.dot(m_diag, xr_b, preferred_element_type=F32)
             + jnp.dot(c_in, st.astype(BF16), preferred_element_type=F32)
             + dskip_ref[0][:, r:r + 1] * xr)
        y_ref[:, r * P:(r + 1) * P] = y
        state_ref[:, r * P:(r + 1) * P] = (st * jnp.exp(last)
                                           + jnp.dot(b_end, xr_b, preferred_element_type=F32))


def ssd(xbc, dt_raw, dt_bias, a_log, d_skip):
    t = xbc.shape[0]
    G, R, L = SSM_GROUPS, SSM_HEADS // SSM_GROUPS, SSD_CHUNK
    gw = R * SSM_HEADDIM
    dt_g = dt_raw.reshape(t, G, R).transpose(1, 0, 2)
    dt_gt = dt_raw.reshape(t, G, R).transpose(1, 2, 0)
    per_head = lambda v: (v.reshape(G, 1, R), v.reshape(G, R, 1))
    bias, bias_t = per_head(dt_bias)
    alog, alog_t = per_head(a_log)
    dskip, _ = per_head(d_skip)
    vec = pl.BlockSpec((1, 1, R), lambda g, c: (g, 0, 0))
    vec_t = pl.BlockSpec((1, R, 1), lambda g, c: (g, 0, 0))
    nb = SSM_INNER // SSM_STATE
    return pl.pallas_call(
        _ssd_kernel,
        out_shape=jax.ShapeDtypeStruct((t, SSM_INNER), F32),
        grid=(G, t // L),
        in_specs=[pl.BlockSpec((L, gw), lambda g, c: (c, g)),
                  pl.BlockSpec((L, SSM_STATE), lambda g, c: (c, nb + g)),
                  pl.BlockSpec((L, SSM_STATE), lambda g, c: (c, nb + G + g)),
                  pl.BlockSpec((1, L, R), lambda g, c: (g, c, 0)),
                  pl.BlockSpec((1, R, L), lambda g, c: (g, 0, c)),
                  vec, vec_t, vec, vec_t, vec],
        out_specs=pl.BlockSpec((L, gw), lambda g, c: (c, g)),
        scratch_shapes=[pltpu.VMEM((SSM_STATE, gw), F32)],
        compiler_params=_params(("parallel", "arbitrary")),
        name="ssd",
    )(xbc, xbc, xbc, dt_g, dt_gt, bias, bias_t, alog, alog_t, dskip)


def _gate_norm_kernel(y_ref, z_ref, g_ref, o_ref):
    yz = y_ref[...] * jax.nn.silu(z_ref[...])
    o_ref[...] = _rms(yz, g_ref[...]).astype(o_ref.dtype)


def gate_norm(y, z, g, *, bt=512):
    t, c = y.shape
    gw = c // SSM_GROUPS
    blk = pl.BlockSpec((bt, gw), lambda i, j: (i, j))
    return pl.pallas_call(
        _gate_norm_kernel,
        out_shape=jax.ShapeDtypeStruct((t, c), BF16),
        grid=(t // bt, SSM_GROUPS),
        in_specs=[blk, blk, pl.BlockSpec((1, gw), lambda i, j: (0, j))],
        out_specs=blk,
        compiler_params=_params(("parallel", "parallel")),
        name="gate_norm",
    )(y, z, g.reshape(1, c))


def kernel(x, mix_norm, ffn_norm, final_norm, ev_w_in, ev_conv_w, ev_sgu_norm, ev_sgu_w, ev_sgu_b, ev_w_out, od_w_in, od_conv_w, od_conv_b, od_dt_bias, od_a_log, od_d, od_norm, od_w_out, peer_wq, peer_keys, peer_u, peer_v):
    h = x[0]

    def peer(xt, i):
        return peer_ffn_t(xt, peer_wq[i].T.astype(BF16), peer_keys[i].astype(BF16),
                          peer_u[i].astype(BF16), peer_v[i].T.astype(BF16))

    hn = rmsnorm(h, mix_norm[0], transpose_out=False)
    proj = matmul(hn, ev_w_in[0].astype(BF16), bm=1024, bn=1024, name="ev_in")
    ycat = evmix(proj, ev_conv_w[0], ev_sgu_norm[0], ev_sgu_w[0], ev_sgu_b[0])
    h = matmul(ycat, ev_w_out[0].astype(BF16), bm=512, bn=1024, res=h, name="ev_out")
    delta_t = peer(rmsnorm(h, ffn_norm[0], transpose_out=True), 0)

    h, hn = resid_norm(h, delta_t, mix_norm[1], want_h=True, norm_dtype=BF16)
    w_in = od_w_in[0]
    z = matmul(hn, w_in[:, :SSM_INNER].astype(BF16), bm=1024, bn=1024, name="od_in_z")
    xbc = matmul(hn, w_in[:, SSM_INNER:SSM_INNER + SSM_CONV_DIM].astype(BF16), bm=1024, bn=1024,
                 name="od_in_xbc")
    dt_raw = matmul(hn, w_in[:, SSM_INNER + SSM_CONV_DIM:].astype(BF16), bm=1024, bn=SSM_HEADS,
                    name="od_in_dt")
    xbc = conv_silu(xbc, od_conv_w[0], od_conv_b[0])
    y = ssd(xbc, dt_raw, od_dt_bias[0], od_a_log[0], od_d[0])
    yn = gate_norm(y, z, od_norm[0])
    h = matmul(yn, od_w_out[0].astype(BF16), bm=512, bn=1024, bk=4096, res=h, name="od_out")
    delta_t = peer(rmsnorm(h, ffn_norm[1], transpose_out=True), 1)

    out = resid_norm(h, delta_t, final_norm, want_h=False, norm_dtype=F32)
    return out[None]
```

```python
import functools

import jax
import jax.numpy as jnp
import numpy as np
from jax import lax
from jax.experimental import pallas as pl
from jax.experimental.pallas import tpu as pltpu

F32 = jnp.float32
BF16 = jnp.bfloat16

EPS = 1e-6
LANES = 128
BF16_ROWS = 16
VMEM_LIMIT = 56 * 1024 * 1024

D_MODEL = 4096
CONV_WIDTH = 2048
CONV_K = 3
SGU_WIDTH = 2048
SGU_HEADS = 16
SGU_BLOCK = 128
SSM_INNER = 8192
SSM_HEADDIM = 64
SSM_HEADS = 128
SSM_GROUPS = 8
SSM_STATE = 128
SSM_CONV_K = 4
SSM_CONV_DIM = SSM_INNER + 2 * SSM_GROUPS * SSM_STATE
SSD_CHUNK = 128
PEER_HEADS = 8
PEER_NKEYS = 128
PEER_HALF = 128
PEER_TOPK = 16
PEER_CAND = [(a, b) for a in range(PEER_TOPK) for b in range(PEER_TOPK)
             if (a + 1) * (b + 1) <= PEER_TOPK]
PEER_NCAND = len(PEER_CAND)
PEER_NCAND_PAD = -(-PEER_NCAND // 8) * 8


def _params(sem):
    return pltpu.CompilerParams(dimension_semantics=sem, vmem_limit_bytes=VMEM_LIMIT)


def _mm_kernel(*refs, nk, has_res):
    if has_res:
        a_ref, b_ref, r_ref, o_ref = refs
    else:
        a_ref, b_ref, o_ref = refs
        r_ref = None

    part = jnp.dot(a_ref[...], b_ref[...], preferred_element_type=F32)
    if nk == 1:
        if has_res:
            part = part + r_ref[...]
        o_ref[...] = part.astype(o_ref.dtype)
    else:
        k = pl.program_id(2)

        @pl.when(k == 0)
        def _():
            o_ref[...] = part + r_ref[...] if has_res else part

        @pl.when(k > 0)
        def _():
            o_ref[...] += part


def matmul(a, b, *, bm, bn, bk=None, out_dtype=F32, res=None, name="mm"):
    m, kdim = a.shape
    _, n = b.shape
    bk = kdim if bk is None else bk
    bm, bn = min(bm, m), min(bn, n)
    nk = kdim // bk
    assert m % bm == 0 and n % bn == 0 and kdim % bk == 0
    assert nk == 1 or out_dtype == F32
    in_specs = [pl.BlockSpec((bm, bk), lambda i, j, k: (i, k)),
                pl.BlockSpec((bk, bn), lambda i, j, k: (k, j))]
    args = [a, b]
    if res is not None:
        in_specs.append(pl.BlockSpec((bm, bn), lambda i, j, k: (i, j)))
        args.append(res)
    return pl.pallas_call(
        functools.partial(_mm_kernel, nk=nk, has_res=res is not None),
        out_shape=jax.ShapeDtypeStruct((m, n), out_dtype),
        grid=(m // bm, n // bn, nk),
        in_specs=in_specs,
        out_specs=pl.BlockSpec((bm, bn), lambda i, j, k: (i, j)),
        compiler_params=_params(("parallel", "parallel", "arbitrary")),
        name=name,
    )(*args)


def _rms(x, g):
    return x * lax.rsqrt(jnp.mean(x * x, axis=-1, keepdims=True) + EPS) * g


def _rmsnorm_kernel(x_ref, g_ref, o_ref, *, transpose_out):
    y = _rms(x_ref[...], g_ref[...])
    if transpose_out:
        y = y.T
    o_ref[...] = y.astype(o_ref.dtype)


def rmsnorm(x, g, *, transpose_out, bt=256):
    t, d = x.shape
    if transpose_out:
        out_shape, out_spec = (d, t), pl.BlockSpec((d, bt), lambda i: (0, i))
    else:
        out_shape, out_spec = (t, d), pl.BlockSpec((bt, d), lambda i: (i, 0))
    return pl.pallas_call(
        functools.partial(_rmsnorm_kernel, transpose_out=transpose_out),
        out_shape=jax.ShapeDtypeStruct(out_shape, BF16),
        grid=(t // bt,),
        in_specs=[pl.BlockSpec((bt, d), lambda i: (i, 0)),
                  pl.BlockSpec((1, d), lambda i: (0, 0))],
        out_specs=out_spec,
        compiler_params=_params(("parallel",)),
        name="rmsnorm_t" if transpose_out else "rmsnorm",
    )(x, g.reshape(1, d))


def _resid_norm_kernel(h_ref, dt_ref, g_ref, *o_refs, want_h):
    h = h_ref[...] + dt_ref[...].T
    if want_h:
        hn_ref, n_ref = o_refs
        hn_ref[...] = h
    else:
        n_ref, = o_refs
    n_ref[...] = _rms(h, g_ref[...]).astype(n_ref.dtype)


def resid_norm(h, delta_t, g, *, want_h, norm_dtype, bt=256):
    t, d = h.shape
    row = pl.BlockSpec((bt, d), lambda i: (i, 0))
    norm_shape = jax.ShapeDtypeStruct((t, d), norm_dtype)
    if want_h:
        out_shape, out_specs = (jax.ShapeDtypeStruct((t, d), F32), norm_shape), (row, row)
    else:
        out_shape, out_specs = norm_shape, row
    return pl.pallas_call(
        functools.partial(_resid_norm_kernel, want_h=want_h),
        out_shape=out_shape,
        grid=(t // bt,),
        in_specs=[row, pl.BlockSpec((d, bt), lambda i: (0, i)),
                  pl.BlockSpec((1, d), lambda i: (0, 0))],
        out_specs=out_specs,
        compiler_params=_params(("parallel",)),
        name="resid_norm",
    )(h, delta_t, g.reshape(1, d))


def _shift_rows(p, prev, k):
    rolled = pltpu.roll(p, k, axis=0)
    head = pltpu.roll(prev, k, axis=0)
    rows = lax.broadcasted_iota(jnp.int32, (8, p.shape[1]), 0)
    fixed = jnp.where(rows < k, head, rolled[:8])
    return jnp.concatenate([fixed, rolled[8:]], axis=0)


def _evmix_kernel(proj_ref, cw_ref, ng_ref, sw_ref, sbt_ref, o_ref, carry_ref):
    @pl.when(pl.program_id(0) == 0)
    def _():
        carry_ref[...] = jnp.zeros_like(carry_ref)

    c = CONV_WIDTH
    gb = proj_ref[:, 0:c]
    p = proj_ref[:, c:2 * c] * proj_ref[:, 2 * c:3 * c]
    prev = carry_ref[...]
    conv = p * cw_ref[CONV_K - 1:CONV_K, :]
    for k in range(1, CONV_K):
        conv = conv + _shift_rows(p, prev, k) * cw_ref[CONV_K - 1 - k:CONV_K - k, :]
    carry_ref[...] = p[p.shape[0] - 8:]
    o_ref[:, 0:c] = (gb * conv).astype(o_ref.dtype)

    u = jax.nn.gelu(proj_ref[:, 3 * c:3 * c + SGU_WIDTH])
    v = jax.nn.gelu(proj_ref[:, 3 * c + SGU_WIDTH:3 * c + 2 * SGU_WIDTH])
    mu = jnp.mean(v, axis=-1, keepdims=True)
    vc = v - mu
    var = jnp.mean(vc * vc, axis=-1, keepdims=True)
    vn = (vc * lax.rsqrt(var + EPS) * ng_ref[...]).astype(BF16)
    q_idx = lax.broadcasted_iota(jnp.int32, (SGU_BLOCK, SGU_BLOCK), 0)
    p_idx = lax.broadcasted_iota(jnp.int32, (SGU_BLOCK, SGU_BLOCK), 1)
    hw = SGU_WIDTH // SGU_HEADS
    for g in range(SGU_HEADS):
        ws = jnp.where(q_idx >= p_idx, sw_ref[g], 0.0).astype(BF16)
        sv = jnp.dot(ws, vn[:, g * hw:(g + 1) * hw], preferred_element_type=F32)
        sv = sv + sbt_ref[:, g:g + 1]
        o_ref[:, c + g * hw:c + (g + 1) * hw] = (u[:, g * hw:(g + 1) * hw] * sv).astype(o_ref.dtype)


def evmix(proj, conv_w, sgu_norm, sgu_w, sgu_b):
    t, width = proj.shape
    bt = SGU_BLOCK
    return pl.pallas_call(
        _evmix_kernel,
        out_shape=jax.ShapeDtypeStruct((t, CONV_WIDTH + SGU_WIDTH), BF16),
        grid=(t // bt,),
        in_specs=[pl.BlockSpec((bt, width), lambda i: (i, 0)),
                  pl.BlockSpec((CONV_K, CONV_WIDTH), lambda i: (0, 0)),
                  pl.BlockSpec((1, SGU_WIDTH), lambda i: (0, 0)),
                  pl.BlockSpec((SGU_HEADS, SGU_BLOCK, SGU_BLOCK), lambda i: (0, 0, 0)),
                  pl.BlockSpec((SGU_BLOCK, SGU_HEADS), lambda i: (0, 0))],
        out_specs=pl.BlockSpec((bt, CONV_WIDTH + SGU_WIDTH), lambda i: (i, 0)),
        scratch_shapes=[pltpu.VMEM((8, CONV_WIDTH), F32)],
        compiler_params=_params(("arbitrary",)),
        name="evmix",
    )(proj, conv_w, sgu_norm.reshape(1, SGU_WIDTH), sgu_w, jnp.transpose(sgu_b))


def _top16(s):
    n, width = s.shape
    rows = lax.broadcasted_iota(jnp.int32, s.shape, 0).astype(F32)
    slot = lax.broadcasted_iota(jnp.int32, (PEER_TOPK, width), 0)
    cur = s
    rank = jnp.full(s.shape, float(PEER_TOPK), F32)
    vals = jnp.zeros((PEER_TOPK, width), F32)
    for r in range(PEER_TOPK):
        m = jnp.max(cur, axis=0, keepdims=True)
        first = jnp.min(jnp.where(cur == m, rows, float(n)), axis=0, keepdims=True)
        sel = rows == first
        rank = jnp.where(sel, float(r), rank)
        cur = jnp.where(sel, -jnp.inf, cur)
        vals = jnp.where(slot == r, m, vals)
    return rank, vals


def _route_kernel(qt_ref, keys_ref, seta_ref, setb_ref, rowsel_ref,
                  lrow_ref, e1_ref, rank2_ref, e2_ref):
    def head(h, carry):
        base = pl.multiple_of(h * (2 * PEER_HALF), 2 * PEER_HALF)
        q1 = qt_ref[pl.ds(base, PEER_HALF), :].astype(BF16)
        q2 = qt_ref[pl.ds(base + PEER_HALF, PEER_HALF), :].astype(BF16)
        s1 = jnp.dot(keys_ref[h, 0], q1, preferred_element_type=F32)
        s2 = jnp.dot(keys_ref[h, 1], q2, preferred_element_type=F32)
        rank1, v1 = _top16(s1)
        rank2, v2 = _top16(s2)
        cand = (jnp.dot(seta_ref[...], v1, preferred_element_type=F32, precision=lax.Precision.HIGHEST)
                + jnp.dot(setb_ref[...], v2, preferred_element_type=F32, precision=lax.Precision.HIGHEST))
        crow = lax.broadcasted_iota(jnp.int32, cand.shape, 0)
        cand = jnp.where(crow < PEER_NCAND, cand, -jnp.inf)
        crank, cvals = _top16(cand)
        chosen = crank < float(PEER_TOPK)
        best = cvals[0:1, :]
        z = jnp.sum(jnp.where(chosen, jnp.exp(cand - best), 0.0), axis=0, keepdims=True)
        length = jnp.dot(rowsel_ref[...], chosen.astype(F32), preferred_element_type=F32)
        lrow = jnp.zeros_like(s1)
        for a in range(PEER_TOPK):
            lrow = jnp.where(rank1 == float(a), length[a:a + 1, :], lrow)
        lrow_ref[h] = lrow
        e1_ref[h] = jnp.exp(s1 - v1[0:1, :])
        rank2_ref[h] = rank2
        e2_ref[h] = jnp.exp(s2 - v2[0:1, :]) / z
        return carry

    lax.fori_loop(0, PEER_HEADS, head, 0)


def peer_route(qt, keys_bf16):
    _, t = qt.shape
    bt = LANES
    seta = np.zeros((PEER_NCAND_PAD, PEER_TOPK), np.float32)
    setb = np.zeros((PEER_NCAND_PAD, PEER_TOPK), np.float32)
    rowsel = np.zeros((PEER_TOPK, PEER_NCAND_PAD), np.float32)
    for k, (a, b) in enumerate(PEER_CAND):
        seta[k, a] = 1.0
        setb[k, b] = 1.0
        rowsel[a, k] = 1.0
    table = jax.ShapeDtypeStruct((PEER_HEADS, PEER_NKEYS, t), F32)
    table16 = jax.ShapeDtypeStruct((PEER_HEADS, PEER_NKEYS, t), BF16)
    tspec = pl.BlockSpec((PEER_HEADS, PEER_NKEYS, bt), lambda i: (0, 0, i))
    const2 = lambda i: (0, 0)
    return pl.pallas_call(
        _route_kernel,
        out_shape=(table, table, table, table),
        grid=(t // bt,),
        in_specs=[pl.BlockSpec((PEER_HEADS * 2 * PEER_HALF, bt), lambda i: (0, i)),
                  pl.BlockSpec((PEER_HEADS, 2, PEER_NKEYS, PEER_HALF), lambda i: (0, 0, 0, 0)),
                  pl.BlockSpec((PEER_NCAND_PAD, PEER_TOPK), const2),
                  pl.BlockSpec((PEER_NCAND_PAD, PEER_TOPK), const2),
                  pl.BlockSpec((PEER_TOPK, PEER_NCAND_PAD), const2)],
        out_specs=(tspec, tspec, tspec, tspec),
        compiler_params=_params(("parallel",)),
        name="peer_route",
    )(qt, keys_bf16, seta, setb, rowsel)


def _peer_act_kernel(u_ref, xt_ref, lrow_ref, e1_ref, rank2_ref, e2_ref, o_ref, *, nsub):
    e = pl.program_id(1)
    bt = o_ref.shape[1]
    nparts = PEER_NKEYS // BF16_ROWS
    first_rows = [[(lrow_ref[h, pl.ds(e * nsub + ii, 1), :], e1_ref[h, pl.ds(e * nsub + ii, 1), :])
                   for ii in range(nsub)] for h in range(PEER_HEADS)]
    for tb in range(bt // LANES):
        cols = slice(tb * LANES, (tb + 1) * LANES)
        acc = [[None] * nparts for _ in range(nsub)]
        for h in range(PEER_HEADS):
            bcast = lambda row: jnp.broadcast_to(row[:, cols], (BF16_ROWS, LANES)).astype(BF16)
            length = [bcast(first_rows[h][ii][0]) for ii in range(nsub)]
            gate1 = [bcast(first_rows[h][ii][1]) for ii in range(nsub)]
            for p in range(nparts):
                rows = slice(p * BF16_ROWS, (p + 1) * BF16_ROWS)
                rank2 = rank2_ref[h, rows, cols].astype(BF16)
                gate2 = e2_ref[h, rows, cols].astype(BF16)
                for ii in range(nsub):
                    contrib = jnp.where(rank2 < length[ii], gate2, jnp.zeros_like(gate2)) * gate1[ii]
                    acc[ii][p] = contrib if acc[ii][p] is None else acc[ii][p] + contrib
        for ii in range(nsub):
            for p in range(nparts):
                o_ref[ii * PEER_NKEYS + p * BF16_ROWS:ii * PEER_NKEYS + (p + 1) * BF16_ROWS, cols] = acc[ii][p]
    for cb in range(bt // 256):
        cols = slice(cb * 256, (cb + 1) * 256)
        hid = jnp.dot(u_ref[...], xt_ref[:, cols], preferred_element_type=F32)
        o_ref[:, cols] = o_ref[:, cols] * jax.nn.gelu(hid).astype(BF16)


def peer_act(u_bf16, xt, tables, *, bt=512, be=512):
    e, d = u_bf16.shape
    _, t = xt.shape
    tspec = pl.BlockSpec((PEER_HEADS, PEER_NKEYS, bt), lambda i, j: (0, 0, i))
    return pl.pallas_call(
        functools.partial(_peer_act_kernel, nsub=be // PEER_NKEYS),
        out_shape=jax.ShapeDtypeStruct((e, t), BF16),
        grid=(t // bt, e // be),
        in_specs=[pl.BlockSpec((be, d), lambda i, j: (j, 0)),
                  pl.BlockSpec((d, bt), lambda i, j: (0, i)),
                  tspec, tspec, tspec, tspec],
        out_specs=pl.BlockSpec((be, bt), lambda i, j: (j, i)),
        compiler_params=_params(("parallel", "arbitrary")),
        name="peer_act",
    )(u_bf16, xt, *tables)


def peer_ffn_t(xt, wq_t, keys_bf16, u_bf16, v_t):
    qt = matmul(wq_t, xt, bm=1024, bn=1024, name="peer_q")
    tables = peer_route(qt, keys_bf16)
    pt = peer_act(u_bf16, xt, tables)
    return matmul(v_t, pt, bm=1024, bn=2048, bk=2048, name="peer_out")


def _conv_silu_kernel(x_ref, w_ref, b_ref, o_ref, carry_ref):
    @pl.when(pl.program_id(1) == 0)
    def _():
        carry_ref[...] = jnp.zeros_like(carry_ref)

    x = x_ref[...]
    prev = carry_ref[...]
    acc = x * w_ref[SSM_CONV_K - 1:SSM_CONV_K, :] + b_ref[...]
    for k in range(1, SSM_CONV_K):
        acc = acc + _shift_rows(x, prev, k) * w_ref[SSM_CONV_K - 1 - k:SSM_CONV_K - k, :]
    carry_ref[...] = x[x.shape[0] - 8:]
    o_ref[...] = jax.nn.silu(acc).astype(o_ref.dtype)


def conv_silu(x, w, b, *, bt=512, bc=1024):
    t, c = x.shape
    return pl.pallas_call(
        _conv_silu_kernel,
        out_shape=jax.ShapeDtypeStruct((t, c), F32),
        grid=(c // bc, t // bt),
        in_specs=[pl.BlockSpec((bt, bc), lambda j, i: (i, j)),
                  pl.BlockSpec((SSM_CONV_K, bc), lambda j, i: (0, j)),
                  pl.BlockSpec((1, bc), lambda j, i: (0, j))],
        out_specs=pl.BlockSpec((bt, bc), lambda j, i: (i, j)),
        scratch_shapes=[pltpu.VMEM((8, bc), F32)],
        compiler_params=_params(("parallel", "arbitrary")),
        name="conv_silu",
    )(x, w, b.reshape(1, c))


def _ssd_kernel(x_ref, b_ref, c_ref, dt_ref, dtt_ref, bias_ref, biast_ref, alog_ref, alogt_ref,
                dskip_ref, y_ref, state_ref, rhs_ref):
    L = SSD_CHUNK
    P = SSM_HEADDIM
    R = SSM_HEADS // SSM_GROUPS
    new_state = []

    @pl.when(pl.program_id(1) == 0)
    def _():
        state_ref[...] = jnp.zeros_like(state_ref)
        rhs_ref[...] = jnp.zeros_like(rhs_ref)

    dt = jax.nn.softplus(dt_ref[0] + bias_ref[0])
    dtt = jax.nn.softplus(dtt_ref[0] + biast_ref[0])
    da = dt * (-jnp.exp(alog_ref[0]))
    dat = dtt * (-jnp.exp(alogt_ref[0]))
    li = lax.broadcasted_iota(jnp.int32, (L, L), 0)
    si = lax.broadcasted_iota(jnp.int32, (L, L), 1)
    causal = li >= si
    acs = jnp.dot(causal.astype(F32), da, preferred_element_type=F32,
                  precision=lax.Precision.HIGHEST)
    acst = jnp.dot(dat, (li <= si).astype(F32), preferred_element_type=F32,
                   precision=lax.Precision.HIGHEST)

    bmat = b_ref[...]
    cmat = c_ref[...]
    cb = lax.dot_general(cmat.astype(BF16), bmat.astype(BF16), (((1,), (1,)), ((), ())),
                         preferred_element_type=F32)
    bt = bmat.T

    lane = lax.broadcasted_iota(jnp.int32, (L, LANES), 1)
    low = lane < P
    zeros_b = jnp.zeros((L, LANES), BF16)
    for tile in range(R // 4):
        for pair in range(2):
            cols = slice(tile * 256 + pair * LANES, tile * 256 + (pair + 1) * LANES)
            xb = x_ref[:, cols].astype(BF16)
            sb = state_ref[:, cols].astype(BF16)
            dst = slice(pair * LANES, (pair + 1) * LANES)
            for k, keep in enumerate((low, ~low)):
                base = (2 * pair + k) * 2 * L
                rhs_ref[tile, base:base + L, dst] = jnp.where(keep, xb, zeros_b)
                rhs_ref[tile, base + L:base + 2 * L, dst] = jnp.where(keep, sb, zeros_b)

    head_of_lane = lax.broadcasted_iota(jnp.int32, (1, R * P), 1) // P
    state_decay = jnp.zeros((1, R * P), F32)
    for tile in range(R // 4):
        top, bottom = [], []
        for r in range(4 * tile, 4 * tile + 4):
            col = acs[:, r:r + 1]
            row = acst[r:r + 1, :]
            last = acst[r:r + 1, L - 1:L]
            decay = jnp.exp(jnp.where(causal, col - row, -jnp.inf))
            dtrow = dtt[r:r + 1, :]
            top.append((cb * decay * dtrow).astype(BF16))
            top.append((cmat * jnp.exp(col)).astype(BF16))
            bottom.append((bt * (jnp.exp(last - row) * dtrow)).astype(BF16))
            bottom.append(zeros_b)
            state_decay = jnp.where(head_of_lane == r, jnp.exp(last), state_decay)
        lhs = jnp.concatenate([jnp.concatenate(top, axis=1), jnp.concatenate(bottom, axis=1)], axis=0)
        res = jnp.dot(lhs, rhs_ref[tile], preferred_element_type=F32)
        cols = slice(tile * 256, (tile + 1) * 256)
        y_ref[:, cols] = res[:L] + dskip_ref[0][:, cols] * x_ref[:, cols]
        new_state.append(res[L:])
    for tile in range(R // 4):
        cols = slice(tile * 256, (tile + 1) * 256)
        state_ref[:, cols] = state_ref[:, cols] * state_decay[:, cols] + new_state[tile]


def ssd(xbc, dt_raw, dt_bias, a_log, d_skip):
    t = xbc.shape[0]
    G, R, L = SSM_GROUPS, SSM_HEADS // SSM_GROUPS, SSD_CHUNK
    gw = R * SSM_HEADDIM
    dt_g = dt_raw.reshape(t, G, R).transpose(1, 0, 2)
    dt_gt = dt_raw.reshape(t, G, R).transpose(1, 2, 0)
    per_head = lambda v: (v.reshape(G, 1, R), v.reshape(G, R, 1))
    bias, bias_t = per_head(dt_bias)
    alog, alog_t = per_head(a_log)
    dskip = jnp.repeat(d_skip, SSM_HEADDIM).reshape(G, 1, gw)
    vec = pl.BlockSpec((1, 1, R), lambda g, c: (g, 0, 0))
    vec_t = pl.BlockSpec((1, R, 1), lambda g, c: (g, 0, 0))
    nb = SSM_INNER // SSM_STATE
    return pl.pallas_call(
        _ssd_kernel,
        out_shape=jax.ShapeDtypeStruct((t, SSM_INNER), F32),
        grid=(G, t // L),
        in_specs=[pl.BlockSpec((L, gw), lambda g, c: (c, g)),
                  pl.BlockSpec((L, SSM_STATE), lambda g, c: (c, nb + g)),
                  pl.BlockSpec((L, SSM_STATE), lambda g, c: (c, nb + G + g)),
                  pl.BlockSpec((1, L, R), lambda g, c: (g, c, 0)),
                  pl.BlockSpec((1, R, L), lambda g, c: (g, 0, c)),
                  vec, vec_t, vec, vec_t, pl.BlockSpec((1, 1, gw), lambda g, c: (g, 0, 0))],
        out_specs=pl.BlockSpec((L, gw), lambda g, c: (c, g)),
        scratch_shapes=[pltpu.VMEM((SSM_STATE, gw), F32),
                        pltpu.VMEM((R // 4, 4 * 2 * L, 256), BF16)],
        compiler_params=_params(("parallel", "arbitrary")),
        name="ssd",
    )(xbc, xbc, xbc, dt_g, dt_gt, bias, bias_t, alog, alog_t, dskip)


def _gate_norm_kernel(y_ref, z_ref, g_ref, o_ref):
    yz = y_ref[...] * jax.nn.silu(z_ref[...])
    o_ref[...] = _rms(yz, g_ref[...]).astype(o_ref.dtype)


def gate_norm(y, z, g, *, bt=512):
    t, c = y.shape
    gw = c // SSM_GROUPS
    blk = pl.BlockSpec((bt, gw), lambda i, j: (i, j))
    return pl.pallas_call(
        _gate_norm_kernel,
        out_shape=jax.ShapeDtypeStruct((t, c), BF16),
        grid=(t // bt, SSM_GROUPS),
        in_specs=[blk, blk, pl.BlockSpec((1, gw), lambda i, j: (0, j))],
        out_specs=blk,
        compiler_params=_params(("parallel", "parallel")),
        name="gate_norm",
    )(y, z, g.reshape(1, c))


def kernel(x, mix_norm, ffn_norm, final_norm, ev_w_in, ev_conv_w, ev_sgu_norm, ev_sgu_w, ev_sgu_b, ev_w_out, od_w_in, od_conv_w, od_conv_b, od_dt_bias, od_a_log, od_d, od_norm, od_w_out, peer_wq, peer_keys, peer_u, peer_v):
    h = x[0]

    def peer(xt, i):
        return peer_ffn_t(xt, peer_wq[i].T.astype(BF16), peer_keys[i].astype(BF16),
                          peer_u[i].astype(BF16), peer_v[i].T.astype(BF16))

    hn = rmsnorm(h, mix_norm[0], transpose_out=False)
    proj = matmul(hn, ev_w_in[0].astype(BF16), bm=1024, bn=1024, name="ev_in")
    ycat = evmix(proj, ev_conv_w[0], ev_sgu_norm[0], ev_sgu_w[0], ev_sgu_b[0])
    h = matmul(ycat, ev_w_out[0].astype(BF16), bm=1024, bn=1024, res=h, name="ev_out")
    delta_t = peer(rmsnorm(h, ffn_norm[0], transpose_out=True), 0)

    h, hn = resid_norm(h, delta_t, mix_norm[1], want_h=True, norm_dtype=BF16)
    w_in = od_w_in[0]
    z = matmul(hn, w_in[:, :SSM_INNER].astype(BF16), bm=1024, bn=1024, name="od_in_z")
    xbc = matmul(hn, w_in[:, SSM_INNER:SSM_INNER + SSM_CONV_DIM].astype(BF16), bm=1024, bn=1024,
                 name="od_in_xbc")
    dt_raw = matmul(hn, w_in[:, SSM_INNER + SSM_CONV_DIM:].astype(BF16), bm=1024, bn=SSM_HEADS,
                    name="od_in_dt")
    xbc = conv_silu(xbc, od_conv_w[0], od_conv_b[0])
    y = ssd(xbc, dt_raw, od_dt_bias[0], od_a_log[0], od_d[0])
    yn = gate_norm(y, z, od_norm[0])
    h = matmul(yn, od_w_out[0].astype(BF16), bm=1024, bn=1024, bk=2048, res=h, name="od_out")
    delta_t = peer(rmsnorm(h, ffn_norm[1], transpose_out=True), 1)

    out = resid_norm(h, delta_t, final_norm, want_h=False, norm_dtype=F32)
    return out[None]
```

```python
import functools

import jax
import jax.numpy as jnp
import numpy as np
from jax import lax
from jax.experimental import pallas as pl
from jax.experimental.pallas import tpu as pltpu

F32 = jnp.float32
BF16 = jnp.bfloat16

EPS = 1e-6
LANES = 128
BF16_ROWS = 16
VMEM_LIMIT = 56 * 1024 * 1024

D_MODEL = 4096
CONV_WIDTH = 2048
CONV_K = 3
SGU_WIDTH = 2048
SGU_HEADS = 16
SGU_BLOCK = 128
SSM_INNER = 8192
SSM_HEADDIM = 64
SSM_HEADS = 128
SSM_GROUPS = 8
SSM_STATE = 128
SSM_CONV_K = 4
SSM_CONV_DIM = SSM_INNER + 2 * SSM_GROUPS * SSM_STATE
SSD_CHUNK = 128
PEER_HEADS = 8
PEER_NKEYS = 128
PEER_HALF = 128
PEER_TOPK = 16
PEER_CAND = [(a, b) for a in range(PEER_TOPK) for b in range(PEER_TOPK)
             if (a + 1) * (b + 1) <= PEER_TOPK]
PEER_NCAND = len(PEER_CAND)
PEER_NCAND_PAD = -(-PEER_NCAND // 8) * 8


def _params(sem):
    return pltpu.CompilerParams(dimension_semantics=sem, vmem_limit_bytes=VMEM_LIMIT)


def _mm_kernel(*refs, nk, has_res):
    if has_res:
        a_ref, b_ref, r_ref, o_ref = refs
    else:
        a_ref, b_ref, o_ref = refs
        r_ref = None

    if nk == 1:
        part = jnp.dot(a_ref[...], b_ref[...], preferred_element_type=F32)
        if has_res:
            part = part + r_ref[...]
        o_ref[...] = part.astype(o_ref.dtype)
    else:
        @pl.when(pl.program_id(2) == 0)
        def _():
            o_ref[...] = r_ref[...] if has_res else jnp.zeros_like(o_ref)

        o_ref[...] += jnp.dot(a_ref[...], b_ref[...], preferred_element_type=F32)


def matmul(a, b, *, bm, bn, bk=None, out_dtype=F32, res=None, name="mm"):
    m, kdim = a.shape
    _, n = b.shape
    bk = kdim if bk is None else bk
    bm, bn = min(bm, m), min(bn, n)
    nk = kdim // bk
    assert m % bm == 0 and n % bn == 0 and kdim % bk == 0
    assert nk == 1 or out_dtype == F32
    in_specs = [pl.BlockSpec((bm, bk), lambda i, j, k: (i, k)),
                pl.BlockSpec((bk, bn), lambda i, j, k: (k, j))]
    args = [a, b]
    if res is not None:
        in_specs.append(pl.BlockSpec((bm, bn), lambda i, j, k: (i, j)))
        args.append(res)
    return pl.pallas_call(
        functools.partial(_mm_kernel, nk=nk, has_res=res is not None),
        out_shape=jax.ShapeDtypeStruct((m, n), out_dtype),
        grid=(m // bm, n // bn, nk),
        in_specs=in_specs,
        out_specs=pl.BlockSpec((bm, bn), lambda i, j, k: (i, j)),
        compiler_params=_params(("parallel", "parallel", "arbitrary")),
        name=name,
    )(*args)


def _rms(x, g):
    return x * lax.rsqrt(jnp.mean(x * x, axis=-1, keepdims=True) + EPS) * g


def _rmsnorm_kernel(x_ref, g_ref, o_ref, *, transpose_out):
    y = _rms(x_ref[...], g_ref[...])
    if transpose_out:
        y = y.T
    o_ref[...] = y.astype(o_ref.dtype)


def rmsnorm(x, g, *, transpose_out, bt=256):
    t, d = x.shape
    if transpose_out:
        out_shape, out_spec = (d, t), pl.BlockSpec((d, bt), lambda i: (0, i))
    else:
        out_shape, out_spec = (t, d), pl.BlockSpec((bt, d), lambda i: (i, 0))
    return pl.pallas_call(
        functools.partial(_rmsnorm_kernel, transpose_out=transpose_out),
        out_shape=jax.ShapeDtypeStruct(out_shape, BF16),
        grid=(t // bt,),
        in_specs=[pl.BlockSpec((bt, d), lambda i: (i, 0)),
                  pl.BlockSpec((1, d), lambda i: (0, 0))],
        out_specs=out_spec,
        compiler_params=_params(("parallel",)),
        name="rmsnorm_t" if transpose_out else "rmsnorm",
    )(x, g.reshape(1, d))


def _resid_norm_kernel(h_ref, dt_ref, g_ref, *o_refs, want_h):
    h = h_ref[...] + dt_ref[...].T
    if want_h:
        hn_ref, n_ref = o_refs
        hn_ref[...] = h
    else:
        n_ref, = o_refs
    n_ref[...] = _rms(h, g_ref[...]).astype(n_ref.dtype)


def resid_norm(h, delta_t, g, *, want_h, norm_dtype, bt=256):
    t, d = h.shape
    row = pl.BlockSpec((bt, d), lambda i: (i, 0))
    norm_shape = jax.ShapeDtypeStruct((t, d), norm_dtype)
    if want_h:
        out_shape, out_specs = (jax.ShapeDtypeStruct((t, d), F32), norm_shape), (row, row)
    else:
        out_shape, out_specs = norm_shape, row
    return pl.pallas_call(
        functools.partial(_resid_norm_kernel, want_h=want_h),
        out_shape=out_shape,
        grid=(t // bt,),
        in_specs=[row, pl.BlockSpec((d, bt), lambda i: (0, i)),
                  pl.BlockSpec((1, d), lambda i: (0, 0))],
        out_specs=out_specs,
        compiler_params=_params(("parallel",)),
        name="resid_norm",
    )(h, delta_t, g.reshape(1, d))


def _shift_rows(p, prev, k):
    rolled = pltpu.roll(p, k, axis=0)
    head = pltpu.roll(prev, k, axis=0)
    rows = lax.broadcasted_iota(jnp.int32, (8, p.shape[1]), 0)
    fixed = jnp.where(rows < k, head, rolled[:8])
    return jnp.concatenate([fixed, rolled[8:]], axis=0)


def _evmix_kernel(proj_ref, cw_ref, ng_ref, sw_ref, sbt_ref, o_ref, carry_ref):
    @pl.when(pl.program_id(0) == 0)
    def _():
        carry_ref[...] = jnp.zeros_like(carry_ref)

    c = CONV_WIDTH
    gb = proj_ref[:, 0:c]
    p = proj_ref[:, c:2 * c] * proj_ref[:, 2 * c:3 * c]
    prev = carry_ref[...]
    conv = p * cw_ref[CONV_K - 1:CONV_K, :]
    for k in range(1, CONV_K):
        conv = conv + _shift_rows(p, prev, k) * cw_ref[CONV_K - 1 - k:CONV_K - k, :]
    carry_ref[...] = p[p.shape[0] - 8:]
    o_ref[:, 0:c] = (gb * conv).astype(o_ref.dtype)

    u = jax.nn.gelu(proj_ref[:, 3 * c:3 * c + SGU_WIDTH])
    v = jax.nn.gelu(proj_ref[:, 3 * c + SGU_WIDTH:3 * c + 2 * SGU_WIDTH])
    mu = jnp.mean(v, axis=-1, keepdims=True)
    vc = v - mu
    var = jnp.mean(vc * vc, axis=-1, keepdims=True)
    vn = (vc * lax.rsqrt(var + EPS) * ng_ref[...]).astype(BF16)
    q_idx = lax.broadcasted_iota(jnp.int32, (SGU_BLOCK, SGU_BLOCK), 0)
    p_idx = lax.broadcasted_iota(jnp.int32, (SGU_BLOCK, SGU_BLOCK), 1)
    hw = SGU_WIDTH // SGU_HEADS
    for g in range(SGU_HEADS):
        ws = jnp.where(q_idx >= p_idx, sw_ref[g], 0.0).astype(BF16)
        sv = jnp.dot(ws, vn[:, g * hw:(g + 1) * hw], preferred_element_type=F32)
        sv = sv + sbt_ref[:, g:g + 1]
        o_ref[:, c + g * hw:c + (g + 1) * hw] = (u[:, g * hw:(g + 1) * hw] * sv).astype(o_ref.dtype)


def evmix(proj, conv_w, sgu_norm, sgu_w, sgu_b):
    t, width = proj.shape
    bt = SGU_BLOCK
    return pl.pallas_call(
        _evmix_kernel,
        out_shape=jax.ShapeDtypeStruct((t, CONV_WIDTH + SGU_WIDTH), BF16),
        grid=(t // bt,),
        in_specs=[pl.BlockSpec((bt, width), lambda i: (i, 0)),
                  pl.BlockSpec((CONV_K, CONV_WIDTH), lambda i: (0, 0)),
                  pl.BlockSpec((1, SGU_WIDTH), lambda i: (0, 0)),
                  pl.BlockSpec((SGU_HEADS, SGU_BLOCK, SGU_BLOCK), lambda i: (0, 0, 0)),
                  pl.BlockSpec((SGU_BLOCK, SGU_HEADS), lambda i: (0, 0))],
        out_specs=pl.BlockSpec((bt, CONV_WIDTH + SGU_WIDTH), lambda i: (i, 0)),
        scratch_shapes=[pltpu.VMEM((8, CONV_WIDTH), F32)],
        compiler_params=_params(("arbitrary",)),
        name="evmix",
    )(proj, conv_w, sgu_norm.reshape(1, SGU_WIDTH), sgu_w, jnp.transpose(sgu_b))


def _top16(s):
    n, width = s.shape
    rows = lax.broadcasted_iota(jnp.int32, s.shape, 0).astype(F32)
    slot = lax.broadcasted_iota(jnp.int32, (PEER_TOPK, width), 0)
    cur = s
    rank = jnp.full(s.shape, float(PEER_TOPK), F32)
    vals = jnp.zeros((PEER_TOPK, width), F32)
    for r in range(PEER_TOPK):
        m = jnp.max(cur, axis=0, keepdims=True)
        first = jnp.min(jnp.where(cur == m, rows, float(n)), axis=0, keepdims=True)
        sel = rows == first
        rank = jnp.where(sel, float(r), rank)
        cur = jnp.where(sel, -jnp.inf, cur)
        vals = jnp.where(slot == r, m, vals)
    return rank, vals


def _top16_distinct(s):
    ax = s.ndim - 2
    slot = lax.broadcasted_iota(jnp.int32, s.shape[:ax] + (PEER_TOPK, s.shape[-1]), ax)
    cur = s
    vals = jnp.zeros(slot.shape, F32)
    for r in range(PEER_TOPK):
        m = jnp.max(cur, axis=ax, keepdims=True)
        cur = jnp.where(cur == m, -jnp.inf, cur)
        vals = jnp.where(slot == r, m, vals)
    last = vals[..., PEER_TOPK - 1:PEER_TOPK, :]
    count = jnp.sum(jnp.where(s >= last, 1.0, 0.0), axis=ax, keepdims=True)
    return vals, count


def _route_kernel(qt_ref, keys_ref, seta_ref, setb_ref, rowsel_ref,
                  lrow_ref, e1_ref, rank2_ref, e2_ref,
                  s_ref, t_ref, n_ref, cand_ref, tc_ref, nc_ref):
    H = PEER_HEADS
    k = float(PEER_TOPK)

    def candidates(v1, v2):
        cand = (jnp.dot(seta_ref[...], v1, preferred_element_type=F32, precision=lax.Precision.HIGHEST)
                + jnp.dot(setb_ref[...], v2, preferred_element_type=F32, precision=lax.Precision.HIGHEST))
        crow = lax.broadcasted_iota(jnp.int32, cand.shape, 0)
        return jnp.where(crow < PEER_NCAND, cand, -jnp.inf)

    for h in range(H):
        for side in range(2):
            q = qt_ref[(2 * h + side) * PEER_HALF:(2 * h + side + 1) * PEER_HALF, :].astype(BF16)
            s_ref[side * H + h] = jnp.dot(keys_ref[h, side], q, preferred_element_type=F32)
    t_all, n_all = _top16_distinct(s_ref[...])
    t_ref[...] = t_all
    n_ref[...] = n_all
    for h in range(H):
        cand_ref[h] = candidates(t_ref[h], t_ref[H + h])
    tc_all, nc_all = _top16_distinct(cand_ref[...])
    tc_ref[...] = tc_all
    nc_ref[...] = nc_all

    def head(h, carry):
        s1 = s_ref[h]
        s2 = s_ref[H + h]

        def finish(cand, chosen, best, lrow_of, rank2):
            z = jnp.sum(jnp.where(chosen, jnp.exp(cand - best), 0.0), axis=0, keepdims=True)
            length = jnp.dot(rowsel_ref[...], chosen.astype(F32), preferred_element_type=F32)
            lrow = jnp.zeros_like(s1)
            for a in range(PEER_TOPK):
                lrow = jnp.where(lrow_of(a), length[a:a + 1, :], lrow)
            lrow_ref[h] = lrow
            rank2_ref[h] = rank2
            e2_ref[h] = jnp.exp(s2 - jnp.max(s2, axis=0, keepdims=True)) / z

        e1_ref[h] = jnp.exp(s1 - jnp.max(s1, axis=0, keepdims=True))
        ties = jnp.max(jnp.abs(n_ref[h] - k) + jnp.abs(n_ref[H + h] - k) + jnp.abs(nc_ref[h] - k))

        @pl.when(ties == 0.0)
        def _():
            t1 = t_ref[h]
            t2 = t_ref[H + h]
            tc = tc_ref[h]
            cand = cand_ref[h]
            rank2 = jnp.full(s2.shape, k, F32)
            for r in range(PEER_TOPK):
                rank2 = jnp.where(s2 == t2[r:r + 1, :], float(r), rank2)
            finish(cand, cand >= tc[PEER_TOPK - 1:PEER_TOPK, :], tc[0:1, :],
                   lambda a: s1 == t1[a:a + 1, :], rank2)

        @pl.when(ties != 0.0)
        def _():
            rank1, v1 = _top16(s1)
            rank2, v2 = _top16(s2)
            cand_exact = candidates(v1, v2)
            crank, cvals = _top16(cand_exact)
            finish(cand_exact, crank < k, cvals[0:1, :], lambda a: rank1 == float(a), rank2)

        return carry

    lax.fori_loop(0, PEER_HEADS, head, 0)


def peer_route(qt, keys_bf16):
    _, t = qt.shape
    bt = LANES
    seta = np.zeros((PEER_NCAND_PAD, PEER_TOPK), np.float32)
    setb = np.zeros((PEER_NCAND_PAD, PEER_TOPK), np.float32)
    rowsel = np.zeros((PEER_TOPK, PEER_NCAND_PAD), np.float32)
    for k, (a, b) in enumerate(PEER_CAND):
        seta[k, a] = 1.0
        setb[k, b] = 1.0
        rowsel[a, k] = 1.0
    table = jax.ShapeDtypeStruct((PEER_HEADS, PEER_NKEYS, t), F32)
    table16 = jax.ShapeDtypeStruct((PEER_HEADS, PEER_NKEYS, t), BF16)
    tspec = pl.BlockSpec((PEER_HEADS, PEER_NKEYS, bt), lambda i: (0, 0, i))
    const2 = lambda i: (0, 0)
    return pl.pallas_call(
        _route_kernel,
        out_shape=(table, table, table, table),
        grid=(t // bt,),
        in_specs=[pl.BlockSpec((PEER_HEADS * 2 * PEER_HALF, bt), lambda i: (0, i)),
                  pl.BlockSpec((PEER_HEADS, 2, PEER_NKEYS, PEER_HALF), lambda i: (0, 0, 0, 0)),
                  pl.BlockSpec((PEER_NCAND_PAD, PEER_TOPK), const2),
                  pl.BlockSpec((PEER_NCAND_PAD, PEER_TOPK), const2),
                  pl.BlockSpec((PEER_TOPK, PEER_NCAND_PAD), const2)],
        out_specs=(tspec, tspec, tspec, tspec),
        scratch_shapes=[pltpu.VMEM((2 * PEER_HEADS, PEER_NKEYS, bt), F32),
                        pltpu.VMEM((2 * PEER_HEADS, PEER_TOPK, bt), F32),
                        pltpu.VMEM((2 * PEER_HEADS, 1, bt), F32),
                        pltpu.VMEM((PEER_HEADS, PEER_NCAND_PAD, bt), F32),
                        pltpu.VMEM((PEER_HEADS, PEER_TOPK, bt), F32),
                        pltpu.VMEM((PEER_HEADS, 1, bt), F32)],
        compiler_params=_params(("parallel",)),
        name="peer_route",
    )(qt, keys_bf16, seta, setb, rowsel)


def _peer_act_kernel(u_ref, xt_ref, lrow_ref, e1_ref, rank2_ref, e2_ref, o_ref, *, nsub):
    e = pl.program_id(1)
    bt = o_ref.shape[1]
    nparts = PEER_NKEYS // BF16_ROWS
    first_rows = [[(lrow_ref[h, pl.ds(e * nsub + ii, 1), :], e1_ref[h, pl.ds(e * nsub + ii, 1), :])
                   for ii in range(nsub)] for h in range(PEER_HEADS)]
    for tb in range(bt // LANES):
        cols = slice(tb * LANES, (tb + 1) * LANES)
        acc = [[None] * nparts for _ in range(nsub)]
        for h in range(PEER_HEADS):
            bcast = lambda row: jnp.broadcast_to(row[:, cols], (BF16_ROWS, LANES)).astype(BF16)
            length = [bcast(first_rows[h][ii][0]) for ii in range(nsub)]
            gate1 = [bcast(first_rows[h][ii][1]) for ii in range(nsub)]
            for p in range(nparts):
                rows = slice(p * BF16_ROWS, (p + 1) * BF16_ROWS)
                rank2 = rank2_ref[h, rows, cols].astype(BF16)
                gate2 = e2_ref[h, rows, cols].astype(BF16)
                for ii in range(nsub):
                    contrib = jnp.where(rank2 < length[ii], gate2, jnp.zeros_like(gate2)) * gate1[ii]
                    acc[ii][p] = contrib if acc[ii][p] is None else acc[ii][p] + contrib
        for ii in range(nsub):
            for p in range(nparts):
                o_ref[ii * PEER_NKEYS + p * BF16_ROWS:ii * PEER_NKEYS + (p + 1) * BF16_ROWS, cols] = acc[ii][p]
    for cb in range(bt // 256):
        cols = slice(cb * 256, (cb + 1) * 256)
        hid = jnp.dot(u_ref[...], xt_ref[:, cols], preferred_element_type=F32)
        o_ref[:, cols] = o_ref[:, cols] * jax.nn.gelu(hid).astype(BF16)


def peer_act(u_bf16, xt, tables, *, bt=512, be=512):
    e, d = u_bf16.shape
    _, t = xt.shape
    tspec = pl.BlockSpec((PEER_HEADS, PEER_NKEYS, bt), lambda i, j: (0, 0, i))
    return pl.pallas_call(
        functools.partial(_peer_act_kernel, nsub=be // PEER_NKEYS),
        out_shape=jax.ShapeDtypeStruct((e, t), BF16),
        grid=(t // bt, e // be),
        in_specs=[pl.BlockSpec((be, d), lambda i, j: (j, 0)),
                  pl.BlockSpec((d, bt), lambda i, j: (0, i)),
                  tspec, tspec, tspec, tspec],
        out_specs=pl.BlockSpec((be, bt), lambda i, j: (j, i)),
        compiler_params=_params(("parallel", "arbitrary")),
        name="peer_act",
    )(u_bf16, xt, *tables)


def peer_ffn_t(xt, wq_t, keys_bf16, u_bf16, v_t):
    qt = matmul(wq_t, xt, bm=1024, bn=1024, name="peer_q")
    tables = peer_route(qt, keys_bf16)
    pt = peer_act(u_bf16, xt, tables)
    return matmul(v_t, pt, bm=1024, bn=2048, bk=2048, name="peer_out")


def _conv_silu_kernel(x_ref, w_ref, b_ref, o_ref, carry_ref):
    @pl.when(pl.program_id(1) == 0)
    def _():
        carry_ref[...] = jnp.zeros_like(carry_ref)

    x = x_ref[...]
    prev = carry_ref[...]
    acc = x * w_ref[SSM_CONV_K - 1:SSM_CONV_K, :] + b_ref[...]
    for k in range(1, SSM_CONV_K):
        acc = acc + _shift_rows(x, prev, k) * w_ref[SSM_CONV_K - 1 - k:SSM_CONV_K - k, :]
    carry_ref[...] = x[x.shape[0] - 8:]
    o_ref[...] = jax.nn.silu(acc).astype(o_ref.dtype)


def conv_silu(x, w, b, *, bt=512, bc=1024):
    t, c = x.shape
    return pl.pallas_call(
        _conv_silu_kernel,
        out_shape=jax.ShapeDtypeStruct((t, c), F32),
        grid=(c // bc, t // bt),
        in_specs=[pl.BlockSpec((bt, bc), lambda j, i: (i, j)),
                  pl.BlockSpec((SSM_CONV_K, bc), lambda j, i: (0, j)),
                  pl.BlockSpec((1, bc), lambda j, i: (0, j))],
        out_specs=pl.BlockSpec((bt, bc), lambda j, i: (i, j)),
        scratch_shapes=[pltpu.VMEM((8, bc), F32)],
        compiler_params=_params(("parallel", "arbitrary")),
        name="conv_silu",
    )(x, w, b.reshape(1, c))


def _ssd_kernel(x_ref, b_ref, c_ref, dt_ref, dtt_ref, bias_ref, biast_ref, alog_ref, alogt_ref,
                dskip_ref, y_ref, state_ref, rhs_ref):
    L = SSD_CHUNK
    P = SSM_HEADDIM
    R = SSM_HEADS // SSM_GROUPS
    new_state = []

    @pl.when(pl.program_id(1) == 0)
    def _():
        state_ref[...] = jnp.zeros_like(state_ref)
        rhs_ref[...] = jnp.zeros_like(rhs_ref)

    dt = jax.nn.softplus(dt_ref[0] + bias_ref[0])
    dtt = jax.nn.softplus(dtt_ref[0] + biast_ref[0])
    da = dt * (-jnp.exp(alog_ref[0]))
    dat = dtt * (-jnp.exp(alogt_ref[0]))
    li = lax.broadcasted_iota(jnp.int32, (L, L), 0)
    si = lax.broadcasted_iota(jnp.int32, (L, L), 1)
    causal = li >= si
    acs = jnp.dot(causal.astype(F32), da, preferred_element_type=F32,
                  precision=lax.Precision.HIGHEST)
    acst = jnp.dot(dat, (li <= si).astype(F32), preferred_element_type=F32,
                   precision=lax.Precision.HIGHEST)

    bmat = b_ref[...]
    cmat = c_ref[...]
    cb = lax.dot_general(cmat.astype(BF16), bmat.astype(BF16), (((1,), (1,)), ((), ())),
                         preferred_element_type=F32)
    bt = bmat.T

    lane = lax.broadcasted_iota(jnp.int32, (L, LANES), 1)
    low = lane < P
    zeros_b = jnp.zeros((L, LANES), BF16)
    for tile in range(R // 4):
        for pair in range(2):
            cols = slice(tile * 256 + pair * LANES, tile * 256 + (pair + 1) * LANES)
            xb = x_ref[:, cols].astype(BF16)
            sb = state_ref[:, cols].astype(BF16)
            dst = slice(pair * LANES, (pair + 1) * LANES)
            for k, keep in enumerate((low, ~low)):
                base = (2 * pair + k) * 2 * L
                rhs_ref[tile, base:base + L, dst] = jnp.where(keep, xb, zeros_b)
                rhs_ref[tile, base + L:base + 2 * L, dst] = jnp.where(keep, sb, zeros_b)

    head_of_lane = lax.broadcasted_iota(jnp.int32, (1, R * P), 1) // P
    state_decay = jnp.zeros((1, R * P), F32)
    for tile in range(R // 4):
        top, bottom = [], []
        for r in range(4 * tile, 4 * tile + 4):
            col = acs[:, r:r + 1]
            row = acst[r:r + 1, :]
            last = acst[r:r + 1, L - 1:L]
            decay = jnp.exp(jnp.where(causal, col - row, -jnp.inf))
            dtrow = dtt[r:r + 1, :]
            top.append((cb * decay * dtrow).astype(BF16))
            top.append((cmat * jnp.exp(col)).astype(BF16))
            bottom.append((bt * (jnp.exp(last - row) * dtrow)).astype(BF16))
            bottom.append(zeros_b)
            state_decay = jnp.where(head_of_lane == r, jnp.exp(last), state_decay)
        lhs = jnp.concatenate([jnp.concatenate(top, axis=1), jnp.concatenate(bottom, axis=1)], axis=0)
        res = jnp.dot(lhs, rhs_ref[tile], preferred_element_type=F32)
        cols = slice(tile * 256, (tile + 1) * 256)
        y_ref[:, cols] = res[:L] + dskip_ref[0][:, cols] * x_ref[:, cols]
        new_state.append(res[L:])
    for tile in range(R // 4):
        cols = slice(tile * 256, (tile + 1) * 256)
        state_ref[:, cols] = state_ref[:, cols] * state_decay[:, cols] + new_state[tile]


def ssd(xbc, dt_raw, dt_bias, a_log, d_skip):
    t = xbc.shape[0]
    G, R, L = SSM_GROUPS, SSM_HEADS // SSM_GROUPS, SSD_CHUNK
    gw = R * SSM_HEADDIM
    dt_g = dt_raw.reshape(t, G, R).transpose(1, 0, 2)
    dt_gt = dt_raw.reshape(t, G, R).transpose(1, 2, 0)
    per_head = lambda v: (v.reshape(G, 1, R), v.reshape(G, R, 1))
    bias, bias_t = per_head(dt_bias)
    alog, alog_t = per_head(a_log)
    dskip = jnp.repeat(d_skip, SSM_HEADDIM).reshape(G, 1, gw)
    vec = pl.BlockSpec((1, 1, R), lambda g, c: (g, 0, 0))
    vec_t = pl.BlockSpec((1, R, 1), lambda g, c: (g, 0, 0))
    nb = SSM_INNER // SSM_STATE
    return pl.pallas_call(
        _ssd_kernel,
        out_shape=jax.ShapeDtypeStruct((t, SSM_INNER), F32),
        grid=(G, t // L),
        in_specs=[pl.BlockSpec((L, gw), lambda g, c: (c, g)),
                  pl.BlockSpec((L, SSM_STATE), lambda g, c: (c, nb + g)),
                  pl.BlockSpec((L, SSM_STATE), lambda g, c: (c, nb + G + g)),
                  pl.BlockSpec((1, L, R), lambda g, c: (g, c, 0)),
                  pl.BlockSpec((1, R, L), lambda g, c: (g, 0, c)),
                  vec, vec_t, vec, vec_t, pl.BlockSpec((1, 1, gw), lambda g, c: (g, 0, 0))],
        out_specs=pl.BlockSpec((L, gw), lambda g, c: (c, g)),
        scratch_shapes=[pltpu.VMEM((SSM_STATE, gw), F32),
                        pltpu.VMEM((R // 4, 4 * 2 * L, 256), BF16)],
        compiler_params=_params(("parallel", "arbitrary")),
        name="ssd",
    )(xbc, xbc, xbc, dt_g, dt_gt, bias, bias_t, alog, alog_t, dskip)


def _gate_norm_kernel(y_ref, z_ref, g_ref, o_ref):
    yz = y_ref[...] * jax.nn.silu(z_ref[...])
    o_ref[...] = _rms(yz, g_ref[...]).astype(o_ref.dtype)


def gate_norm(y, z, g, *, bt=512):
    t, c = y.shape
    gw = c // SSM_GROUPS
    blk = pl.BlockSpec((bt, gw), lambda i, j: (i, j))
    return pl.pallas_call(
        _gate_norm_kernel,
        out_shape=jax.ShapeDtypeStruct((t, c), BF16),
        grid=(t // bt, SSM_GROUPS),
        in_specs=[blk, blk, pl.BlockSpec((1, gw), lambda i, j: (0, j))],
        out_specs=blk,
        compiler_params=_params(("parallel", "parallel")),
        name="gate_norm",
    )(y, z, g.reshape(1, c))


def kernel(x, mix_norm, ffn_norm, final_norm, ev_w_in, ev_conv_w, ev_sgu_norm, ev_sgu_w, ev_sgu_b, ev_w_out, od_w_in, od_conv_w, od_conv_b, od_dt_bias, od_a_log, od_d, od_norm, od_w_out, peer_wq, peer_keys, peer_u, peer_v):
    h = x[0]

    def peer(xt, i):
        return peer_ffn_t(xt, peer_wq[i].T.astype(BF16), peer_keys[i].astype(BF16),
                          peer_u[i].astype(BF16), peer_v[i].T.astype(BF16))

    hn = rmsnorm(h, mix_norm[0], transpose_out=False)
    proj = matmul(hn, ev_w_in[0].astype(BF16), bm=1024, bn=1024, name="ev_in")
    ycat = evmix(proj, ev_conv_w[0], ev_sgu_norm[0], ev_sgu_w[0], ev_sgu_b[0])
    h = matmul(ycat, ev_w_out[0].astype(BF16), bm=1024, bn=1024, res=h, name="ev_out")
    delta_t = peer(rmsnorm(h, ffn_norm[0], transpose_out=True), 0)

    h, hn = resid_norm(h, delta_t, mix_norm[1], want_h=True, norm_dtype=BF16)
    w_in = od_w_in[0]
    z = matmul(hn, w_in[:, :SSM_INNER].astype(BF16), bm=1024, bn=1024, name="od_in_z")
    xbc = matmul(hn, w_in[:, SSM_INNER:SSM_INNER + SSM_CONV_DIM].astype(BF16), bm=1024, bn=1024,
                 name="od_in_xbc")
    dt_raw = matmul(hn, w_in[:, SSM_INNER + SSM_CONV_DIM:].astype(BF16), bm=1024, bn=SSM_HEADS,
                    name="od_in_dt")
    xbc = conv_silu(xbc, od_conv_w[0], od_conv_b[0])
    y = ssd(xbc, dt_raw, od_dt_bias[0], od_a_log[0], od_d[0])
    yn = gate_norm(y, z, od_norm[0])
    h = matmul(yn, od_w_out[0].astype(BF16), bm=1024, bn=1024, bk=2048, res=h, name="od_out")
    delta_t = peer(rmsnorm(h, ffn_norm[1], transpose_out=True), 1)

    out = resid_norm(h, delta_t, final_norm, want_h=False, norm_dtype=F32)
    return out[None]
```

```python
import functools

import jax
import jax.numpy as jnp
import numpy as np
from jax import lax
from jax.experimental import pallas as pl
from jax.experimental.pallas import tpu as pltpu

F32 = jnp.float32
BF16 = jnp.bfloat16

EPS = 1e-6
LANES = 128
BF16_ROWS = 16
VMEM_LIMIT = 56 * 1024 * 1024

D_MODEL = 4096
CONV_WIDTH = 2048
CONV_K = 3
SGU_WIDTH = 2048
SGU_HEADS = 16
SGU_BLOCK = 128
SSM_INNER = 8192
SSM_HEADDIM = 64
SSM_HEADS = 128
SSM_GROUPS = 8
SSM_STATE = 128
SSM_CONV_K = 4
SSM_CONV_DIM = SSM_INNER + 2 * SSM_GROUPS * SSM_STATE
SSD_CHUNK = 128
PEER_HEADS = 8
PEER_NKEYS = 128
PEER_HALF = 128
PEER_TOPK = 16
PEER_CAND = [(a, b) for a in range(PEER_TOPK) for b in range(PEER_TOPK)
             if (a + 1) * (b + 1) <= PEER_TOPK]
PEER_NCAND = len(PEER_CAND)
PEER_NCAND_PAD = -(-PEER_NCAND // 8) * 8


def _params(sem):
    return pltpu.CompilerParams(dimension_semantics=sem, vmem_limit_bytes=VMEM_LIMIT)


def _mm_kernel(*refs, nk, has_res):
    if has_res:
        a_ref, b_ref, r_ref, o_ref = refs
    else:
        a_ref, b_ref, o_ref = refs
        r_ref = None

    if nk == 1:
        part = jnp.dot(a_ref[...], b_ref[...], preferred_element_type=F32)
        if has_res:
            part = part + r_ref[...]
        o_ref[...] = part.astype(o_ref.dtype)
    else:
        @pl.when(pl.program_id(2) == 0)
        def _():
            o_ref[...] = r_ref[...] if has_res else jnp.zeros_like(o_ref)

        o_ref[...] += jnp.dot(a_ref[...], b_ref[...], preferred_element_type=F32)


def matmul(a, b, *, bm, bn, bk=None, out_dtype=F32, res=None, name="mm"):
    m, kdim = a.shape
    _, n = b.shape
    bk = kdim if bk is None else bk
    bm, bn = min(bm, m), min(bn, n)
    nk = kdim // bk
    assert m % bm == 0 and n % bn == 0 and kdim % bk == 0
    assert nk == 1 or out_dtype == F32
    in_specs = [pl.BlockSpec((bm, bk), lambda i, j, k: (i, k)),
                pl.BlockSpec((bk, bn), lambda i, j, k: (k, j))]
    args = [a, b]
    if res is not None:
        in_specs.append(pl.BlockSpec((bm, bn), lambda i, j, k: (i, j)))
        args.append(res)
    return pl.pallas_call(
        functools.partial(_mm_kernel, nk=nk, has_res=res is not None),
        out_shape=jax.ShapeDtypeStruct((m, n), out_dtype),
        grid=(m // bm, n // bn, nk),
        in_specs=in_specs,
        out_specs=pl.BlockSpec((bm, bn), lambda i, j, k: (i, j)),
        compiler_params=_params(("parallel", "parallel", "arbitrary")),
        name=name,
    )(*args)


def _rms(x, g):
    return x * lax.rsqrt(jnp.mean(x * x, axis=-1, keepdims=True) + EPS) * g


def _rmsnorm_kernel(x_ref, g_ref, o_ref, *, transpose_out):
    y = _rms(x_ref[...], g_ref[...])
    if transpose_out:
        y = y.T
    o_ref[...] = y.astype(o_ref.dtype)


def rmsnorm(x, g, *, transpose_out, bt=256):
    t, d = x.shape
    if transpose_out:
        out_shape, out_spec = (d, t), pl.BlockSpec((d, bt), lambda i: (0, i))
    else:
        out_shape, out_spec = (t, d), pl.BlockSpec((bt, d), lambda i: (i, 0))
    return pl.pallas_call(
        functools.partial(_rmsnorm_kernel, transpose_out=transpose_out),
        out_shape=jax.ShapeDtypeStruct(out_shape, BF16),
        grid=(t // bt,),
        in_specs=[pl.BlockSpec((bt, d), lambda i: (i, 0)),
                  pl.BlockSpec((1, d), lambda i: (0, 0))],
        out_specs=out_spec,
        compiler_params=_params(("parallel",)),
        name="rmsnorm_t" if transpose_out else "rmsnorm",
    )(x, g.reshape(1, d))


def _resid_norm_kernel(h_ref, dt_ref, g_ref, *o_refs, want_h):
    h = h_ref[...] + dt_ref[...].T
    if want_h:
        hn_ref, n_ref = o_refs
        hn_ref[...] = h
    else:
        n_ref, = o_refs
    n_ref[...] = _rms(h, g_ref[...]).astype(n_ref.dtype)


def resid_norm(h, delta_t, g, *, want_h, norm_dtype, bt=256):
    t, d = h.shape
    row = pl.BlockSpec((bt, d), lambda i: (i, 0))
    norm_shape = jax.ShapeDtypeStruct((t, d), norm_dtype)
    if want_h:
        out_shape, out_specs = (jax.ShapeDtypeStruct((t, d), F32), norm_shape), (row, row)
    else:
        out_shape, out_specs = norm_shape, row
    return pl.pallas_call(
        functools.partial(_resid_norm_kernel, want_h=want_h),
        out_shape=out_shape,
        grid=(t // bt,),
        in_specs=[row, pl.BlockSpec((d, bt), lambda i: (0, i)),
                  pl.BlockSpec((1, d), lambda i: (0, 0))],
        out_specs=out_specs,
        compiler_params=_params(("parallel",)),
        name="resid_norm",
    )(h, delta_t, g.reshape(1, d))


def _shift_rows(p, prev, k):
    rolled = pltpu.roll(p, k, axis=0)
    head = pltpu.roll(prev, k, axis=0)
    rows = lax.broadcasted_iota(jnp.int32, (8, p.shape[1]), 0)
    fixed = jnp.where(rows < k, head, rolled[:8])
    return jnp.concatenate([fixed, rolled[8:]], axis=0)


def _evmix_kernel(proj_ref, cw_ref, ng_ref, sw_ref, sbt_ref, o_ref, carry_ref):
    @pl.when(pl.program_id(0) == 0)
    def _():
        carry_ref[...] = jnp.zeros_like(carry_ref)

    c = CONV_WIDTH
    gb = proj_ref[:, 0:c]
    p = proj_ref[:, c:2 * c] * proj_ref[:, 2 * c:3 * c]
    prev = carry_ref[...]
    conv = p * cw_ref[CONV_K - 1:CONV_K, :]
    for k in range(1, CONV_K):
        conv = conv + _shift_rows(p, prev, k) * cw_ref[CONV_K - 1 - k:CONV_K - k, :]
    carry_ref[...] = p[p.shape[0] - 8:]
    o_ref[:, 0:c] = (gb * conv).astype(o_ref.dtype)

    u = jax.nn.gelu(proj_ref[:, 3 * c:3 * c + SGU_WIDTH])
    v = jax.nn.gelu(proj_ref[:, 3 * c + SGU_WIDTH:3 * c + 2 * SGU_WIDTH])
    mu = jnp.mean(v, axis=-1, keepdims=True)
    vc = v - mu
    var = jnp.mean(vc * vc, axis=-1, keepdims=True)
    vn = (vc * lax.rsqrt(var + EPS) * ng_ref[...]).astype(BF16)
    q_idx = lax.broadcasted_iota(jnp.int32, (SGU_BLOCK, SGU_BLOCK), 0)
    p_idx = lax.broadcasted_iota(jnp.int32, (SGU_BLOCK, SGU_BLOCK), 1)
    hw = SGU_WIDTH // SGU_HEADS
    for g in range(SGU_HEADS):
        ws = jnp.where(q_idx >= p_idx, sw_ref[g], 0.0).astype(BF16)
        sv = jnp.dot(ws, vn[:, g * hw:(g + 1) * hw], preferred_element_type=F32)
        sv = sv + sbt_ref[:, g:g + 1]
        o_ref[:, c + g * hw:c + (g + 1) * hw] = (u[:, g * hw:(g + 1) * hw] * sv).astype(o_ref.dtype)


def evmix(proj, conv_w, sgu_norm, sgu_w, sgu_b):
    t, width = proj.shape
    bt = SGU_BLOCK
    return pl.pallas_call(
        _evmix_kernel,
        out_shape=jax.ShapeDtypeStruct((t, CONV_WIDTH + SGU_WIDTH), BF16),
        grid=(t // bt,),
        in_specs=[pl.BlockSpec((bt, width), lambda i: (i, 0)),
                  pl.BlockSpec((CONV_K, CONV_WIDTH), lambda i: (0, 0)),
                  pl.BlockSpec((1, SGU_WIDTH), lambda i: (0, 0)),
                  pl.BlockSpec((SGU_HEADS, SGU_BLOCK, SGU_BLOCK), lambda i: (0, 0, 0)),
                  pl.BlockSpec((SGU_BLOCK, SGU_HEADS), lambda i: (0, 0))],
        out_specs=pl.BlockSpec((bt, CONV_WIDTH + SGU_WIDTH), lambda i: (i, 0)),
        scratch_shapes=[pltpu.VMEM((8, CONV_WIDTH), F32)],
        compiler_params=_params(("arbitrary",)),
        name="evmix",
    )(proj, conv_w, sgu_norm.reshape(1, SGU_WIDTH), sgu_w, jnp.transpose(sgu_b))


def _top16(s):
    n, width = s.shape
    rows = lax.broadcasted_iota(jnp.int32, s.shape, 0).astype(F32)
    slot = lax.broadcasted_iota(jnp.int32, (PEER_TOPK, width), 0)
    cur = s
    rank = jnp.full(s.shape, float(PEER_TOPK), F32)
    vals = jnp.zeros((PEER_TOPK, width), F32)
    for r in range(PEER_TOPK):
        m = jnp.max(cur, axis=0, keepdims=True)
        first = jnp.min(jnp.where(cur == m, rows, float(n)), axis=0, keepdims=True)
        sel = rows == first
        rank = jnp.where(sel, float(r), rank)
        cur = jnp.where(sel, -jnp.inf, cur)
        vals = jnp.where(slot == r, m, vals)
    return rank, vals


def _top16_distinct(s):
    ax = s.ndim - 2
    slot = lax.broadcasted_iota(jnp.int32, s.shape[:ax] + (PEER_TOPK, s.shape[-1]), ax)
    cur = s
    vals = jnp.zeros(slot.shape, F32)
    for r in range(PEER_TOPK):
        m = jnp.max(cur, axis=ax, keepdims=True)
        cur = jnp.where(cur == m, -jnp.inf, cur)
        vals = jnp.where(slot == r, m, vals)
    last = vals[..., PEER_TOPK - 1:PEER_TOPK, :]
    count = jnp.sum(jnp.where(s >= last, 1.0, 0.0), axis=ax, keepdims=True)
    return vals, count


def _route_kernel(qt_ref, keys_ref, seta_ref, setb_ref, rowsel_ref,
                  lrow_ref, e1_ref, rank2_ref, e2_ref,
                  s_ref, t_ref, n_ref, cand_ref, tc_ref, nc_ref):
    H = PEER_HEADS
    k = float(PEER_TOPK)

    def candidates(v1, v2):
        cand = (jnp.dot(seta_ref[...], v1, preferred_element_type=F32, precision=lax.Precision.HIGHEST)
                + jnp.dot(setb_ref[...], v2, preferred_element_type=F32, precision=lax.Precision.HIGHEST))
        crow = lax.broadcasted_iota(jnp.int32, cand.shape, 0)
        return jnp.where(crow < PEER_NCAND, cand, -jnp.inf)

    for h in range(H):
        for side in range(2):
            q = qt_ref[(2 * h + side) * PEER_HALF:(2 * h + side + 1) * PEER_HALF, :].astype(BF16)
            s_ref[side * H + h] = jnp.dot(keys_ref[h, side], q, preferred_element_type=F32)
    t_all, n_all = _top16_distinct(s_ref[...])
    t_ref[...] = t_all
    n_ref[...] = n_all
    for h in range(H):
        cand_ref[h] = candidates(t_ref[h], t_ref[H + h])
    tc_all, nc_all = _top16_distinct(cand_ref[...])
    tc_ref[...] = tc_all
    nc_ref[...] = nc_all

    def head(h, carry):
        s1 = s_ref[h]
        s2 = s_ref[H + h]

        def finish(cand, chosen, best, lrow_of, rank2):
            z = jnp.sum(jnp.where(chosen, jnp.exp(cand - best), 0.0), axis=0, keepdims=True)
            length = jnp.dot(rowsel_ref[...], chosen.astype(F32), preferred_element_type=F32)
            lrow = jnp.zeros_like(s1)
            for a in range(PEER_TOPK):
                lrow = jnp.where(lrow_of(a), length[a:a + 1, :], lrow)
            lrow_ref[h] = lrow
            rank2_ref[h] = rank2
            e2_ref[h] = jnp.exp(s2 - jnp.max(s2, axis=0, keepdims=True)) / z

        e1_ref[h] = jnp.exp(s1 - jnp.max(s1, axis=0, keepdims=True))
        ties = jnp.max(jnp.abs(n_ref[h] - k) + jnp.abs(n_ref[H + h] - k) + jnp.abs(nc_ref[h] - k))

        @pl.when(ties == 0.0)
        def _():
            t1 = t_ref[h]
            t2 = t_ref[H + h]
            tc = tc_ref[h]
            cand = cand_ref[h]
            rank2 = jnp.full(s2.shape, k, F32)
            for r in range(PEER_TOPK):
                rank2 = jnp.where(s2 == t2[r:r + 1, :], float(r), rank2)
            finish(cand, cand >= tc[PEER_TOPK - 1:PEER_TOPK, :], tc[0:1, :],
                   lambda a: s1 == t1[a:a + 1, :], rank2)

        @pl.when(ties != 0.0)
        def _():
            rank1, v1 = _top16(s1)
            rank2, v2 = _top16(s2)
            cand_exact = candidates(v1, v2)
            crank, cvals = _top16(cand_exact)
            finish(cand_exact, crank < k, cvals[0:1, :], lambda a: rank1 == float(a), rank2)

        return carry

    lax.fori_loop(0, PEER_HEADS, head, 0)


def peer_route(qt, keys_bf16):
    _, t = qt.shape
    bt = LANES
    seta = np.zeros((PEER_NCAND_PAD, PEER_TOPK), np.float32)
    setb = np.zeros((PEER_NCAND_PAD, PEER_TOPK), np.float32)
    rowsel = np.zeros((PEER_TOPK, PEER_NCAND_PAD), np.float32)
    for k, (a, b) in enumerate(PEER_CAND):
        seta[k, a] = 1.0
        setb[k, b] = 1.0
        rowsel[a, k] = 1.0
    table = jax.ShapeDtypeStruct((PEER_HEADS, PEER_NKEYS, t), F32)
    table16 = jax.ShapeDtypeStruct((PEER_HEADS, PEER_NKEYS, t), BF16)
    tspec = pl.BlockSpec((PEER_HEADS, PEER_NKEYS, bt), lambda i: (0, 0, i))
    const2 = lambda i: (0, 0)
    return pl.pallas_call(
        _route_kernel,
        out_shape=(table, table, table, table),
        grid=(t // bt,),
        in_specs=[pl.BlockSpec((PEER_HEADS * 2 * PEER_HALF, bt), lambda i: (0, i)),
                  pl.BlockSpec((PEER_HEADS, 2, PEER_NKEYS, PEER_HALF), lambda i: (0, 0, 0, 0)),
                  pl.BlockSpec((PEER_NCAND_PAD, PEER_TOPK), const2),
                  pl.BlockSpec((PEER_NCAND_PAD, PEER_TOPK), const2),
                  pl.BlockSpec((PEER_TOPK, PEER_NCAND_PAD), const2)],
        out_specs=(tspec, tspec, tspec, tspec),
        scratch_shapes=[pltpu.VMEM((2 * PEER_HEADS, PEER_NKEYS, bt), F32),
                        pltpu.VMEM((2 * PEER_HEADS, PEER_TOPK, bt), F32),
                        pltpu.VMEM((2 * PEER_HEADS, 1, bt), F32),
                        pltpu.VMEM((PEER_HEADS, PEER_NCAND_PAD, bt), F32),
                        pltpu.VMEM((PEER_HEADS, PEER_TOPK, bt), F32),
                        pltpu.VMEM((PEER_HEADS, 1, bt), F32)],
        compiler_params=_params(("parallel",)),
        name="peer_route",
    )(qt, keys_bf16, seta, setb, rowsel)


def _peer_act_kernel(u_ref, xt_ref, o_ref):
    hid = jnp.dot(u_ref[...], xt_ref[...], preferred_element_type=F32)
    o_ref[...] = jax.nn.gelu(hid).astype(o_ref.dtype)


def peer_act(u_bf16, xt, *, bt=512, be=2048):
    e, d = u_bf16.shape
    _, t = xt.shape
    return pl.pallas_call(
        _peer_act_kernel,
        out_shape=jax.ShapeDtypeStruct((e, t), BF16),
        grid=(t // bt, e // be),
        in_specs=[pl.BlockSpec((be, d), lambda i, j: (j, 0)),
                  pl.BlockSpec((d, bt), lambda i, j: (0, i))],
        out_specs=pl.BlockSpec((be, bt), lambda i, j: (j, i)),
        compiler_params=_params(("parallel", "arbitrary")),
        name="peer_act",
    )(u_bf16, xt)


def _peer_out_kernel(vt_ref, a_ref, lrow_ref, e1_ref, rank2_ref, e2_ref, o_ref, p_ref, *, nsub):
    e = pl.program_id(1)
    bt = o_ref.shape[1]
    nparts = PEER_NKEYS // BF16_ROWS

    @pl.when(e == 0)
    def _():
        o_ref[...] = jnp.zeros_like(o_ref)

    first_rows = [[(lrow_ref[h, pl.ds(e * nsub + ii, 1), :], e1_ref[h, pl.ds(e * nsub + ii, 1), :])
                   for ii in range(nsub)] for h in range(PEER_HEADS)]
    for tb in range(bt // LANES):
        cols = slice(tb * LANES, (tb + 1) * LANES)
        acc = [[None] * nparts for _ in range(nsub)]
        for h in range(PEER_HEADS):
            bcast = lambda row: jnp.broadcast_to(row[:, cols], (BF16_ROWS, LANES)).astype(BF16)
            length = [bcast(first_rows[h][ii][0]) for ii in range(nsub)]
            gate1 = [bcast(first_rows[h][ii][1]) for ii in range(nsub)]
            for p in range(nparts):
                rows = slice(p * BF16_ROWS, (p + 1) * BF16_ROWS)
                rank2 = rank2_ref[h, rows, cols].astype(BF16)
                gate2 = e2_ref[h, rows, cols].astype(BF16)
                for ii in range(nsub):
                    contrib = jnp.where(rank2 < length[ii], gate2, jnp.zeros_like(gate2)) * gate1[ii]
                    acc[ii][p] = contrib if acc[ii][p] is None else acc[ii][p] + contrib
        for ii in range(nsub):
            for p in range(nparts):
                rows = slice(ii * PEER_NKEYS + p * BF16_ROWS, ii * PEER_NKEYS + (p + 1) * BF16_ROWS)
                p_ref[rows, cols] = acc[ii][p] * a_ref[rows, cols]
    o_ref[...] += jnp.dot(vt_ref[...], p_ref[...], preferred_element_type=F32)


def peer_out(v_t, at, tables, *, bt=512, be=512):
    d, e = v_t.shape
    _, t = at.shape
    tspec = pl.BlockSpec((PEER_HEADS, PEER_NKEYS, bt), lambda i, j: (0, 0, i))
    return pl.pallas_call(
        functools.partial(_peer_out_kernel, nsub=be // PEER_NKEYS),
        out_shape=jax.ShapeDtypeStruct((d, t), F32),
        grid=(t // bt, e // be),
        in_specs=[pl.BlockSpec((d, be), lambda i, j: (0, j)),
                  pl.BlockSpec((be, bt), lambda i, j: (j, i)),
                  tspec, tspec, tspec, tspec],
        out_specs=pl.BlockSpec((d, bt), lambda i, j: (0, i)),
        scratch_shapes=[pltpu.VMEM((be, bt), BF16)],
        compiler_params=_params(("parallel", "arbitrary")),
        name="peer_out",
    )(v_t, at, *tables)


def peer_ffn_t(xt, wq_t, keys_bf16, u_bf16, v_t):
    qt = matmul(wq_t, xt, bm=1024, bn=1024, name="peer_q")
    tables = peer_route(qt, keys_bf16)
    return peer_out(v_t, peer_act(u_bf16, xt), tables)


def _conv_silu_kernel(x_ref, w_ref, b_ref, o_ref, carry_ref):
    @pl.when(pl.program_id(1) == 0)
    def _():
        carry_ref[...] = jnp.zeros_like(carry_ref)

    x = x_ref[...]
    prev = carry_ref[...]
    acc = x * w_ref[SSM_CONV_K - 1:SSM_CONV_K, :] + b_ref[...]
    for k in range(1, SSM_CONV_K):
        acc = acc + _shift_rows(x, prev, k) * w_ref[SSM_CONV_K - 1 - k:SSM_CONV_K - k, :]
    carry_ref[...] = x[x.shape[0] - 8:]
    o_ref[...] = jax.nn.silu(acc).astype(o_ref.dtype)


def conv_silu(x, w, b, *, bt=512, bc=1024):
    t, c = x.shape
    return pl.pallas_call(
        _conv_silu_kernel,
        out_shape=jax.ShapeDtypeStruct((t, c), BF16),
        grid=(c // bc, t // bt),
        in_specs=[pl.BlockSpec((bt, bc), lambda j, i: (i, j)),
                  pl.BlockSpec((SSM_CONV_K, bc), lambda j, i: (0, j)),
                  pl.BlockSpec((1, bc), lambda j, i: (0, j))],
        out_specs=pl.BlockSpec((bt, bc), lambda j, i: (i, j)),
        scratch_shapes=[pltpu.VMEM((8, bc), F32)],
        compiler_params=_params(("parallel", "arbitrary")),
        name="conv_silu",
    )(x, w, b.reshape(1, c))


def _ssd_kernel(x_ref, b_ref, c_ref, dt_ref, dtt_ref, bias_ref, biast_ref, alog_ref, alogt_ref,
                dskip_ref, z_ref, g_ref, y_ref, state_ref, rhs_ref, gated_ref):
    L = SSD_CHUNK
    P = SSM_HEADDIM
    R = SSM_HEADS // SSM_GROUPS
    new_state = []
    sumsq = None

    @pl.when(pl.program_id(1) == 0)
    def _():
        state_ref[...] = jnp.zeros_like(state_ref)
        rhs_ref[...] = jnp.zeros_like(rhs_ref)

    dt = jax.nn.softplus(dt_ref[0] + bias_ref[0])
    dtt = jax.nn.softplus(dtt_ref[0] + biast_ref[0])
    da = dt * (-jnp.exp(alog_ref[0]))
    dat = dtt * (-jnp.exp(alogt_ref[0]))
    li = lax.broadcasted_iota(jnp.int32, (L, L), 0)
    si = lax.broadcasted_iota(jnp.int32, (L, L), 1)
    causal = li >= si
    acs = jnp.dot(causal.astype(F32), da, preferred_element_type=F32,
                  precision=lax.Precision.HIGHEST)
    acst = jnp.dot(dat, (li <= si).astype(F32), preferred_element_type=F32,
                   precision=lax.Precision.HIGHEST)

    bmat = b_ref[...].astype(F32)
    cmat = c_ref[...].astype(F32)
    cb = lax.dot_general(cmat.astype(BF16), bmat.astype(BF16), (((1,), (1,)), ((), ())),
                         preferred_element_type=F32)
    bt = bmat.T

    lane = lax.broadcasted_iota(jnp.int32, (L, LANES), 1)
    low = lane < P
    zeros_b = jnp.zeros((L, LANES), BF16)
    for tile in range(R // 4):
        for pair in range(2):
            cols = slice(tile * 256 + pair * LANES, tile * 256 + (pair + 1) * LANES)
            xb = x_ref[:, cols].astype(BF16)
            sb = state_ref[:, cols].astype(BF16)
            dst = slice(pair * LANES, (pair + 1) * LANES)
            for k, keep in enumerate((low, ~low)):
                base = (2 * pair + k) * 2 * L
                rhs_ref[tile, base:base + L, dst] = jnp.where(keep, xb, zeros_b)
                rhs_ref[tile, base + L:base + 2 * L, dst] = jnp.where(keep, sb, zeros_b)

    head_of_lane = lax.broadcasted_iota(jnp.int32, (1, R * P), 1) // P
    state_decay = jnp.zeros((1, R * P), F32)
    for tile in range(R // 4):
        top, bottom = [], []
        for r in range(4 * tile, 4 * tile + 4):
            col = acs[:, r:r + 1]
            row = acst[r:r + 1, :]
            last = acst[r:r + 1, L - 1:L]
            decay = jnp.exp(jnp.where(causal, col - row, -jnp.inf))
            dtrow = dtt[r:r + 1, :]
            top.append((cb * decay * dtrow).astype(BF16))
            top.append((cmat * jnp.exp(col)).astype(BF16))
            bottom.append((bt * (jnp.exp(last - row) * dtrow)).astype(BF16))
            bottom.append(zeros_b)
            state_decay = jnp.where(head_of_lane == r, jnp.exp(last), state_decay)
        lhs = jnp.concatenate([jnp.concatenate(top, axis=1), jnp.concatenate(bottom, axis=1)], axis=0)
        res = jnp.dot(lhs, rhs_ref[tile], preferred_element_type=F32)
        cols = slice(tile * 256, (tile + 1) * 256)
        y = res[:L] + dskip_ref[0][:, cols] * x_ref[:, cols].astype(F32)
        gated = y * jax.nn.silu(z_ref[:, cols])
        gated_ref[:, cols] = gated
        part = jnp.sum(gated * gated, axis=-1, keepdims=True)
        sumsq = part if sumsq is None else sumsq + part
        new_state.append(res[L:])
    for tile in range(R // 4):
        cols = slice(tile * 256, (tile + 1) * 256)
        state_ref[:, cols] = state_ref[:, cols] * state_decay[:, cols] + new_state[tile]
    scale = lax.rsqrt(sumsq * (1.0 / (R * P)) + EPS)
    y_ref[...] = (gated_ref[...] * scale * g_ref[...]).astype(y_ref.dtype)


def ssd(xbc, dt_raw, dt_bias, a_log, d_skip, z, norm_g):
    t = xbc.shape[0]
    G, R, L = SSM_GROUPS, SSM_HEADS // SSM_GROUPS, SSD_CHUNK
    gw = R * SSM_HEADDIM
    dt_g = dt_raw.reshape(t, G, R).transpose(1, 0, 2)
    dt_gt = dt_raw.reshape(t, G, R).transpose(1, 2, 0)
    per_head = lambda v: (v.reshape(G, 1, R), v.reshape(G, R, 1))
    bias, bias_t = per_head(dt_bias)
    alog, alog_t = per_head(a_log)
    dskip = jnp.repeat(d_skip, SSM_HEADDIM).reshape(G, 1, gw)
    vec = pl.BlockSpec((1, 1, R), lambda g, c: (g, 0, 0))
    vec_t = pl.BlockSpec((1, R, 1), lambda g, c: (g, 0, 0))
    nb = SSM_INNER // SSM_STATE
    return pl.pallas_call(
        _ssd_kernel,
        out_shape=jax.ShapeDtypeStruct((t, SSM_INNER), BF16),
        grid=(G, t // L),
        in_specs=[pl.BlockSpec((L, gw), lambda g, c: (c, g)),
                  pl.BlockSpec((L, SSM_STATE), lambda g, c: (c, nb + g)),
                  pl.BlockSpec((L, SSM_STATE), lambda g, c: (c, nb + G + g)),
                  pl.BlockSpec((1, L, R), lambda g, c: (g, c, 0)),
                  pl.BlockSpec((1, R, L), lambda g, c: (g, 0, c)),
                  vec, vec_t, vec, vec_t, pl.BlockSpec((1, 1, gw), lambda g, c: (g, 0, 0)),
                  pl.BlockSpec((L, gw), lambda g, c: (c, g)),
                  pl.BlockSpec((1, gw), lambda g, c: (0, g))],
        out_specs=pl.BlockSpec((L, gw), lambda g, c: (c, g)),
        scratch_shapes=[pltpu.VMEM((SSM_STATE, gw), F32),
                        pltpu.VMEM((R // 4, 4 * 2 * L, 256), BF16),
                        pltpu.VMEM((L, gw), F32)],
        compiler_params=_params(("parallel", "arbitrary")),
        name="ssd",
    )(xbc, xbc, xbc, dt_g, dt_gt, bias, bias_t, alog, alog_t, dskip, z, norm_g.reshape(1, SSM_INNER))


def kernel(x, mix_norm, ffn_norm, final_norm, ev_w_in, ev_conv_w, ev_sgu_norm, ev_sgu_w, ev_sgu_b, ev_w_out, od_w_in, od_conv_w, od_conv_b, od_dt_bias, od_a_log, od_d, od_norm, od_w_out, peer_wq, peer_keys, peer_u, peer_v):
    h = x[0]

    def peer(xt, i):
        return peer_ffn_t(xt, peer_wq[i].T.astype(BF16), peer_keys[i].astype(BF16),
                          peer_u[i].astype(BF16), peer_v[i].T.astype(BF16))

    hn = rmsnorm(h, mix_norm[0], transpose_out=False)
    proj = matmul(hn, ev_w_in[0].astype(BF16), bm=1024, bn=1024, name="ev_in")
    ycat = evmix(proj, ev_conv_w[0], ev_sgu_norm[0], ev_sgu_w[0], ev_sgu_b[0])
    h = matmul(ycat, ev_w_out[0].astype(BF16), bm=1024, bn=1024, res=h, name="ev_out")
    delta_t = peer(rmsnorm(h, ffn_norm[0], transpose_out=True), 0)

    h, hn = resid_norm(h, delta_t, mix_norm[1], want_h=True, norm_dtype=BF16)
    w_in = od_w_in[0]
    z = matmul(hn, w_in[:, :SSM_INNER].astype(BF16), bm=1024, bn=1024, name="od_in_z")
    xbc = matmul(hn, w_in[:, SSM_INNER:SSM_INNER + SSM_CONV_DIM].astype(BF16), bm=1024, bn=1024,
                 name="od_in_xbc")
    dt_raw = matmul(hn, w_in[:, SSM_INNER + SSM_CONV_DIM:].astype(BF16), bm=1024, bn=SSM_HEADS,
                    name="od_in_dt")
    xbc = conv_silu(xbc, od_conv_w[0], od_conv_b[0])
    yn = ssd(xbc, dt_raw, od_dt_bias[0], od_a_log[0], od_d[0], z, od_norm[0])
    h = matmul(yn, od_w_out[0].astype(BF16), bm=1024, bn=1024, bk=2048, res=h, name="od_out")
    delta_t = peer(rmsnorm(h, ffn_norm[1], transpose_out=True), 1)

    out = resid_norm(h, delta_t, final_norm, want_h=False, norm_dtype=F32)
    return out[None]
```

```python
import functools
import math

import jax
import jax.numpy as jnp
import numpy as np
from jax import lax
from jax.experimental import pallas as pl
from jax.experimental.pallas import tpu as pltpu

F32 = jnp.float32
BF16 = jnp.bfloat16

EPS = 1e-6
LANES = 128
BF16_ROWS = 16
VMEM_LIMIT = 56 * 1024 * 1024

D_MODEL = 4096
CONV_WIDTH = 2048
CONV_K = 3
SGU_WIDTH = 2048
SGU_HEADS = 16
SGU_BLOCK = 128
SSM_INNER = 8192
SSM_HEADDIM = 64
SSM_HEADS = 128
SSM_GROUPS = 8
SSM_STATE = 128
SSM_CONV_K = 4
SSM_CONV_DIM = SSM_INNER + 2 * SSM_GROUPS * SSM_STATE
SSD_CHUNK = 128
PEER_HEADS = 8
PEER_NKEYS = 128
PEER_HALF = 128
PEER_TOPK = 16
PEER_CAND = [(a, b) for a in range(PEER_TOPK) for b in range(PEER_TOPK)
             if (a + 1) * (b + 1) <= PEER_TOPK]
PEER_NCAND = len(PEER_CAND)
PEER_NCAND_PAD = -(-PEER_NCAND // 8) * 8


def _params(sem):
    return pltpu.CompilerParams(dimension_semantics=sem, vmem_limit_bytes=VMEM_LIMIT)


def _mm_kernel(*refs, nk, has_res):
    if has_res:
        a_ref, b_ref, r_ref, o_ref = refs
    else:
        a_ref, b_ref, o_ref = refs
        r_ref = None

    if nk == 1:
        part = jnp.dot(a_ref[...], b_ref[...], preferred_element_type=F32)
        if has_res:
            part = part + r_ref[...]
        o_ref[...] = part.astype(o_ref.dtype)
    else:
        @pl.when(pl.program_id(2) == 0)
        def _():
            o_ref[...] = r_ref[...] if has_res else jnp.zeros_like(o_ref)

        o_ref[...] += jnp.dot(a_ref[...], b_ref[...], preferred_element_type=F32)


def matmul(a, b, *, bm, bn, bk=None, out_dtype=F32, res=None, name="mm"):
    m, kdim = a.shape
    _, n = b.shape
    bk = kdim if bk is None else bk
    bm, bn = min(bm, m), min(bn, n)
    nk = kdim // bk
    assert m % bm == 0 and n % bn == 0 and kdim % bk == 0
    assert nk == 1 or out_dtype == F32
    in_specs = [pl.BlockSpec((bm, bk), lambda i, j, k: (i, k)),
                pl.BlockSpec((bk, bn), lambda i, j, k: (k, j))]
    args = [a, b]
    if res is not None:
        in_specs.append(pl.BlockSpec((bm, bn), lambda i, j, k: (i, j)))
        args.append(res)
    return pl.pallas_call(
        functools.partial(_mm_kernel, nk=nk, has_res=res is not None),
        out_shape=jax.ShapeDtypeStruct((m, n), out_dtype),
        grid=(m // bm, n // bn, nk),
        in_specs=in_specs,
        out_specs=pl.BlockSpec((bm, bn), lambda i, j, k: (i, j)),
        compiler_params=_params(("parallel", "parallel", "arbitrary")),
        name=name,
    )(*args)


_GELU_K0 = -2.0 * math.sqrt(2.0 / math.pi) * math.log2(math.e)
_GELU_K1 = _GELU_K0 * 0.044715


def _gelu_tanh(x):
    return x / (1.0 + jnp.exp2(x * (_GELU_K0 + _GELU_K1 * (x * x))))


def _rms(x, g):
    return x * lax.rsqrt(jnp.mean(x * x, axis=-1, keepdims=True) + EPS) * g


def _rmsnorm_kernel(x_ref, g_ref, o_ref, *, transpose_out):
    y = _rms(x_ref[...], g_ref[...])
    if transpose_out:
        y = y.T
    o_ref[...] = y.astype(o_ref.dtype)


def rmsnorm(x, g, *, transpose_out, bt=256):
    t, d = x.shape
    if transpose_out:
        out_shape, out_spec = (d, t), pl.BlockSpec((d, bt), lambda i: (0, i))
    else:
        out_shape, out_spec = (t, d), pl.BlockSpec((bt, d), lambda i: (i, 0))
    return pl.pallas_call(
        functools.partial(_rmsnorm_kernel, transpose_out=transpose_out),
        out_shape=jax.ShapeDtypeStruct(out_shape, BF16),
        grid=(t // bt,),
        in_specs=[pl.BlockSpec((bt, d), lambda i: (i, 0)),
                  pl.BlockSpec((1, d), lambda i: (0, 0))],
        out_specs=out_spec,
        compiler_params=_params(("parallel",)),
        name="rmsnorm_t" if transpose_out else "rmsnorm",
    )(x, g.reshape(1, d))


def _resid_norm_kernel(h_ref, dt_ref, g_ref, *o_refs, want_h):
    h = h_ref[...] + dt_ref[...].T
    if want_h:
        hn_ref, n_ref = o_refs
        hn_ref[...] = h
    else:
        n_ref, = o_refs
    n_ref[...] = _rms(h, g_ref[...]).astype(n_ref.dtype)


def resid_norm(h, delta_t, g, *, want_h, norm_dtype, bt=256):
    t, d = h.shape
    row = pl.BlockSpec((bt, d), lambda i: (i, 0))
    norm_shape = jax.ShapeDtypeStruct((t, d), norm_dtype)
    if want_h:
        out_shape, out_specs = (jax.ShapeDtypeStruct((t, d), F32), norm_shape), (row, row)
    else:
        out_shape, out_specs = norm_shape, row
    return pl.pallas_call(
        functools.partial(_resid_norm_kernel, want_h=want_h),
        out_shape=out_shape,
        grid=(t // bt,),
        in_specs=[row, pl.BlockSpec((d, bt), lambda i: (0, i)),
                  pl.BlockSpec((1, d), lambda i: (0, 0))],
        out_specs=out_specs,
        compiler_params=_params(("parallel",)),
        name="resid_norm",
    )(h, delta_t, g.reshape(1, d))


def _shift_rows(p, prev, k):
    rolled = pltpu.roll(p, k, axis=0)
    head = pltpu.roll(prev, k, axis=0)
    rows = lax.broadcasted_iota(jnp.int32, (8, p.shape[1]), 0)
    fixed = jnp.where(rows < k, head, rolled[:8])
    return jnp.concatenate([fixed, rolled[8:]], axis=0)


def _evmix_kernel(proj_ref, cw_ref, ng_ref, sw_ref, sbt_ref, o_ref, carry_ref):
    @pl.when(pl.program_id(0) == 0)
    def _():
        carry_ref[...] = jnp.zeros_like(carry_ref)

    c = CONV_WIDTH
    gb = proj_ref[:, 0:c]
    p = proj_ref[:, c:2 * c] * proj_ref[:, 2 * c:3 * c]
    prev = carry_ref[...]
    conv = p * cw_ref[CONV_K - 1:CONV_K, :]
    for k in range(1, CONV_K):
        conv = conv + _shift_rows(p, prev, k) * cw_ref[CONV_K - 1 - k:CONV_K - k, :]
    carry_ref[...] = p[p.shape[0] - 8:]
    o_ref[:, 0:c] = (gb * conv).astype(o_ref.dtype)

    u = _gelu_tanh(proj_ref[:, 3 * c:3 * c + SGU_WIDTH])
    v = _gelu_tanh(proj_ref[:, 3 * c + SGU_WIDTH:3 * c + 2 * SGU_WIDTH])
    mu = jnp.mean(v, axis=-1, keepdims=True)
    vc = v - mu
    var = jnp.mean(vc * vc, axis=-1, keepdims=True)
    vn = (vc * lax.rsqrt(var + EPS) * ng_ref[...]).astype(BF16)
    q_idx = lax.broadcasted_iota(jnp.int32, (SGU_BLOCK, SGU_BLOCK), 0)
    p_idx = lax.broadcasted_iota(jnp.int32, (SGU_BLOCK, SGU_BLOCK), 1)
    hw = SGU_WIDTH // SGU_HEADS
    for g in range(SGU_HEADS):
        ws = jnp.where(q_idx >= p_idx, sw_ref[g], 0.0).astype(BF16)
        sv = jnp.dot(ws, vn[:, g * hw:(g + 1) * hw], preferred_element_type=F32)
        sv = sv + sbt_ref[:, g:g + 1]
        o_ref[:, c + g * hw:c + (g + 1) * hw] = (u[:, g * hw:(g + 1) * hw] * sv).astype(o_ref.dtype)


def evmix(proj, conv_w, sgu_norm, sgu_w, sgu_b):
    t, width = proj.shape
    bt = SGU_BLOCK
    return pl.pallas_call(
        _evmix_kernel,
        out_shape=jax.ShapeDtypeStruct((t, CONV_WIDTH + SGU_WIDTH), BF16),
        grid=(t // bt,),
        in_specs=[pl.BlockSpec((bt, width), lambda i: (i, 0)),
                  pl.BlockSpec((CONV_K, CONV_WIDTH), lambda i: (0, 0)),
                  pl.BlockSpec((1, SGU_WIDTH), lambda i: (0, 0)),
                  pl.BlockSpec((SGU_HEADS, SGU_BLOCK, SGU_BLOCK), lambda i: (0, 0, 0)),
                  pl.BlockSpec((SGU_BLOCK, SGU_HEADS), lambda i: (0, 0))],
        out_specs=pl.BlockSpec((bt, CONV_WIDTH + SGU_WIDTH), lambda i: (i, 0)),
        scratch_shapes=[pltpu.VMEM((8, CONV_WIDTH), F32)],
        compiler_params=_params(("arbitrary",)),
        name="evmix",
    )(proj, conv_w, sgu_norm.reshape(1, SGU_WIDTH), sgu_w, jnp.transpose(sgu_b))


def _top16(s):
    n, width = s.shape
    rows = lax.broadcasted_iota(jnp.int32, s.shape, 0).astype(F32)
    slot = lax.broadcasted_iota(jnp.int32, (PEER_TOPK, width), 0)
    cur = s
    rank = jnp.full(s.shape, float(PEER_TOPK), F32)
    vals = jnp.zeros((PEER_TOPK, width), F32)
    for r in range(PEER_TOPK):
        m = jnp.max(cur, axis=0, keepdims=True)
        first = jnp.min(jnp.where(cur == m, rows, float(n)), axis=0, keepdims=True)
        sel = rows == first
        rank = jnp.where(sel, float(r), rank)
        cur = jnp.where(sel, -jnp.inf, cur)
        vals = jnp.where(slot == r, m, vals)
    return rank, vals


def _top16_distinct(s):
    ax = s.ndim - 2
    slot = lax.broadcasted_iota(jnp.int32, s.shape[:ax] + (PEER_TOPK, s.shape[-1]), ax)
    cur = s
    vals = jnp.zeros(slot.shape, F32)
    for r in range(PEER_TOPK):
        m = jnp.max(cur, axis=ax, keepdims=True)
        cur = jnp.where(cur == m, -jnp.inf, cur)
        vals = jnp.where(slot == r, m, vals)
    last = vals[..., PEER_TOPK - 1:PEER_TOPK, :]
    count = jnp.sum(jnp.where(s >= last, 1.0, 0.0), axis=ax, keepdims=True)
    return vals, count


def _route_kernel(qt_ref, keys_ref, seta_ref, setb_ref, rowsel_ref,
                  lrow_ref, e1_ref, rank2_ref, e2_ref,
                  s_ref, t_ref, n_ref, cand_ref, tc_ref, nc_ref):
    H = PEER_HEADS
    k = float(PEER_TOPK)

    def candidates(v1, v2):
        cand = (jnp.dot(seta_ref[...], v1, preferred_element_type=F32, precision=lax.Precision.HIGHEST)
                + jnp.dot(setb_ref[...], v2, preferred_element_type=F32, precision=lax.Precision.HIGHEST))
        crow = lax.broadcasted_iota(jnp.int32, cand.shape, 0)
        return jnp.where(crow < PEER_NCAND, cand, -jnp.inf)

    for h in range(H):
        for side in range(2):
            q = qt_ref[(2 * h + side) * PEER_HALF:(2 * h + side + 1) * PEER_HALF, :].astype(BF16)
            s_ref[side * H + h] = jnp.dot(keys_ref[h, side], q, preferred_element_type=F32)
    t_all, n_all = _top16_distinct(s_ref[...])
    t_ref[...] = t_all
    n_ref[...] = n_all
    for h in range(H):
        cand_ref[h] = candidates(t_ref[h], t_ref[H + h])
    tc_all, nc_all = _top16_distinct(cand_ref[...])
    tc_ref[...] = tc_all
    nc_ref[...] = nc_all

    def head(h, carry):
        s1 = s_ref[h]
        s2 = s_ref[H + h]

        def finish(cand, chosen, best, lrow_of, rank2):
            z = jnp.sum(jnp.where(chosen, jnp.exp(cand - best), 0.0), axis=0, keepdims=True)
            length = jnp.dot(rowsel_ref[...], chosen.astype(F32), preferred_element_type=F32)
            lrow = jnp.zeros_like(s1)
            for a in range(PEER_TOPK):
                lrow = jnp.where(lrow_of(a), length[a:a + 1, :], lrow)
            lrow_ref[h] = lrow
            rank2_ref[h] = rank2
            e2_ref[h] = jnp.exp(s2 - jnp.max(s2, axis=0, keepdims=True)) / z

        e1_ref[h] = jnp.exp(s1 - jnp.max(s1, axis=0, keepdims=True))
        ties = jnp.max(jnp.abs(n_ref[h] - k) + jnp.abs(n_ref[H + h] - k) + jnp.abs(nc_ref[h] - k))

        @pl.when(ties == 0.0)
        def _():
            t1 = t_ref[h]
            t2 = t_ref[H + h]
            tc = tc_ref[h]
            cand = cand_ref[h]
            rank2 = jnp.full(s2.shape, k, F32)
            for r in range(PEER_TOPK):
                rank2 = jnp.where(s2 == t2[r:r + 1, :], float(r), rank2)
            finish(cand, cand >= tc[PEER_TOPK - 1:PEER_TOPK, :], tc[0:1, :],
                   lambda a: s1 == t1[a:a + 1, :], rank2)

        @pl.when(ties != 0.0)
        def _():
            rank1, v1 = _top16(s1)
            rank2, v2 = _top16(s2)
            cand_exact = candidates(v1, v2)
            crank, cvals = _top16(cand_exact)
            finish(cand_exact, crank < k, cvals[0:1, :], lambda a: rank1 == float(a), rank2)

        return carry

    lax.fori_loop(0, PEER_HEADS, head, 0)


def peer_route(qt, keys_bf16):
    _, t = qt.shape
    bt = LANES
    seta = np.zeros((PEER_NCAND_PAD, PEER_TOPK), np.float32)
    setb = np.zeros((PEER_NCAND_PAD, PEER_TOPK), np.float32)
    rowsel = np.zeros((PEER_TOPK, PEER_NCAND_PAD), np.float32)
    for k, (a, b) in enumerate(PEER_CAND):
        seta[k, a] = 1.0
        setb[k, b] = 1.0
        rowsel[a, k] = 1.0
    table = jax.ShapeDtypeStruct((PEER_HEADS, PEER_NKEYS, t), F32)
    table16 = jax.ShapeDtypeStruct((PEER_HEADS, PEER_NKEYS, t), BF16)
    tspec = pl.BlockSpec((PEER_HEADS, PEER_NKEYS, bt), lambda i: (0, 0, i))
    const2 = lambda i: (0, 0)
    return pl.pallas_call(
        _route_kernel,
        out_shape=(table, table, table, table),
        grid=(t // bt,),
        in_specs=[pl.BlockSpec((PEER_HEADS * 2 * PEER_HALF, bt), lambda i: (0, i)),
                  pl.BlockSpec((PEER_HEADS, 2, PEER_NKEYS, PEER_HALF), lambda i: (0, 0, 0, 0)),
                  pl.BlockSpec((PEER_NCAND_PAD, PEER_TOPK), const2),
                  pl.BlockSpec((PEER_NCAND_PAD, PEER_TOPK), const2),
                  pl.BlockSpec((PEER_TOPK, PEER_NCAND_PAD), const2)],
        out_specs=(tspec, tspec, tspec, tspec),
        scratch_shapes=[pltpu.VMEM((2 * PEER_HEADS, PEER_NKEYS, bt), F32),
                        pltpu.VMEM((2 * PEER_HEADS, PEER_TOPK, bt), F32),
                        pltpu.VMEM((2 * PEER_HEADS, 1, bt), F32),
                        pltpu.VMEM((PEER_HEADS, PEER_NCAND_PAD, bt), F32),
                        pltpu.VMEM((PEER_HEADS, PEER_TOPK, bt), F32),
                        pltpu.VMEM((PEER_HEADS, 1, bt), F32)],
        compiler_params=_params(("parallel",)),
        name="peer_route",
    )(qt, keys_bf16, seta, setb, rowsel)


def _peer_act_kernel(u_ref, xt_ref, o_ref):
    hid = jnp.dot(u_ref[...], xt_ref[...], preferred_element_type=F32)
    o_ref[...] = _gelu_tanh(hid).astype(o_ref.dtype)


def peer_act(u_bf16, xt, *, bt=512, be=2048):
    e, d = u_bf16.shape
    _, t = xt.shape
    return pl.pallas_call(
        _peer_act_kernel,
        out_shape=jax.ShapeDtypeStruct((e, t), BF16),
        grid=(t // bt, e // be),
        in_specs=[pl.BlockSpec((be, d), lambda i, j: (j, 0)),
                  pl.BlockSpec((d, bt), lambda i, j: (0, i))],
        out_specs=pl.BlockSpec((be, bt), lambda i, j: (j, i)),
        compiler_params=_params(("parallel", "arbitrary")),
        name="peer_act",
    )(u_bf16, xt)


def _peer_out_kernel(vt_ref, a_ref, lrow_ref, e1_ref, rank2_ref, e2_ref, o_ref, p_ref, *, nsub):
    e = pl.program_id(1)
    bt = o_ref.shape[1]
    nparts = PEER_NKEYS // BF16_ROWS

    @pl.when(e == 0)
    def _():
        o_ref[...] = jnp.zeros_like(o_ref)

    first_rows = [[(lrow_ref[h, pl.ds(e * nsub + ii, 1), :], e1_ref[h, pl.ds(e * nsub + ii, 1), :])
                   for ii in range(nsub)] for h in range(PEER_HEADS)]
    for tb in range(bt // LANES):
        cols = slice(tb * LANES, (tb + 1) * LANES)
        acc = [[None] * nparts for _ in range(nsub)]
        for h in range(PEER_HEADS):
            bcast = lambda row: jnp.broadcast_to(row[:, cols], (BF16_ROWS, LANES)).astype(BF16)
            length = [bcast(first_rows[h][ii][0]) for ii in range(nsub)]
            gate1 = [bcast(first_rows[h][ii][1]) for ii in range(nsub)]
            for p in range(nparts):
                rows = slice(p * BF16_ROWS, (p + 1) * BF16_ROWS)
                rank2 = rank2_ref[h, rows, cols].astype(BF16)
                gate2 = e2_ref[h, rows, cols].astype(BF16)
                for ii in range(nsub):
                    contrib = jnp.where(rank2 < length[ii], gate2, jnp.zeros_like(gate2)) * gate1[ii]
                    acc[ii][p] = contrib if acc[ii][p] is None else acc[ii][p] + contrib
        for ii in range(nsub):
            for p in range(nparts):
                rows = slice(ii * PEER_NKEYS + p * BF16_ROWS, ii * PEER_NKEYS + (p + 1) * BF16_ROWS)
                p_ref[rows, cols] = acc[ii][p] * a_ref[rows, cols]
    o_ref[...] += jnp.dot(vt_ref[...], p_ref[...], preferred_element_type=F32)


def peer_out(v_t, at, tables, *, bt=512, be=512):
    d, e = v_t.shape
    _, t = at.shape
    tspec = pl.BlockSpec((PEER_HEADS, PEER_NKEYS, bt), lambda i, j: (0, 0, i))
    return pl.pallas_call(
        functools.partial(_peer_out_kernel, nsub=be // PEER_NKEYS),
        out_shape=jax.ShapeDtypeStruct((d, t), F32),
        grid=(t // bt, e // be),
        in_specs=[pl.BlockSpec((d, be), lambda i, j: (0, j)),
                  pl.BlockSpec((be, bt), lambda i, j: (j, i)),
                  tspec, tspec, tspec, tspec],
        out_specs=pl.BlockSpec((d, bt), lambda i, j: (0, i)),
        scratch_shapes=[pltpu.VMEM((be, bt), BF16)],
        compiler_params=_params(("parallel", "arbitrary")),
        name="peer_out",
    )(v_t, at, *tables)


def peer_ffn_t(xt, wq_t, keys_bf16, u_bf16, v_t):
    qt = matmul(wq_t, xt, bm=1024, bn=1024, name="peer_q")
    tables = peer_route(qt, keys_bf16)
    return peer_out(v_t, peer_act(u_bf16, xt), tables)


def _conv_silu_kernel(x_ref, w_ref, b_ref, o_ref, carry_ref):
    @pl.when(pl.program_id(1) == 0)
    def _():
        carry_ref[...] = jnp.zeros_like(carry_ref)

    x = x_ref[...]
    prev = carry_ref[...]
    acc = x * w_ref[SSM_CONV_K - 1:SSM_CONV_K, :] + b_ref[...]
    for k in range(1, SSM_CONV_K):
        acc = acc + _shift_rows(x, prev, k) * w_ref[SSM_CONV_K - 1 - k:SSM_CONV_K - k, :]
    carry_ref[...] = x[x.shape[0] - 8:]
    o_ref[...] = jax.nn.silu(acc).astype(o_ref.dtype)


def conv_silu(x, w, b, *, bt=512, bc=1024):
    t, c = x.shape
    return pl.pallas_call(
        _conv_silu_kernel,
        out_shape=jax.ShapeDtypeStruct((t, c), BF16),
        grid=(c // bc, t // bt),
        in_specs=[pl.BlockSpec((bt, bc), lambda j, i: (i, j)),
                  pl.BlockSpec((SSM_CONV_K, bc), lambda j, i: (0, j)),
                  pl.BlockSpec((1, bc), lambda j, i: (0, j))],
        out_specs=pl.BlockSpec((bt, bc), lambda j, i: (i, j)),
        scratch_shapes=[pltpu.VMEM((8, bc), F32)],
        compiler_params=_params(("parallel", "arbitrary")),
        name="conv_silu",
    )(x, w, b.reshape(1, c))


def _dt_prep_kernel(dt_ref, bias_ref, alog_ref, acs_ref, acst_ref, rowp_ref):
    L = SSD_CHUNK
    dt = jax.nn.softplus(dt_ref[...] + bias_ref[...])
    da = dt * (-jnp.exp(alog_ref[...]))
    li = lax.broadcasted_iota(jnp.int32, (L, L), 0)
    si = lax.broadcasted_iota(jnp.int32, (L, L), 1)
    acs = jnp.dot((li >= si).astype(F32), da, preferred_element_type=F32,
                  precision=lax.Precision.HIGHEST) * math.log2(math.e)
    acs_ref[...] = acs
    acst_ref[...] = acs.T
    rowp_ref[...] = (acs - jnp.log2(dt)).T


def dt_prep(dt_raw, dt_bias, a_log):
    t, heads = dt_raw.shape
    L = SSD_CHUNK
    vec = pl.BlockSpec((1, heads), lambda c: (0, 0))
    tr = pl.BlockSpec((heads, L), lambda c: (0, c))
    return pl.pallas_call(
        _dt_prep_kernel,
        out_shape=(jax.ShapeDtypeStruct((t, heads), F32), jax.ShapeDtypeStruct((heads, t), F32),
                   jax.ShapeDtypeStruct((heads, t), F32)),
        grid=(t // L,),
        in_specs=[pl.BlockSpec((L, heads), lambda c: (c, 0)), vec, vec],
        out_specs=(pl.BlockSpec((L, heads), lambda c: (c, 0)), tr, tr),
        compiler_params=_params(("parallel",)),
        name="dt_prep",
    )(dt_raw, dt_bias.reshape(1, heads), a_log.reshape(1, heads))


def _ssd_kernel(x_ref, b_ref, c_ref, acs_ref, acst_ref, rowp_ref,
                dskip_ref, z_ref, g_ref, y_ref, state_ref, rhs_ref, gated_ref):
    L = SSD_CHUNK
    P = SSM_HEADDIM
    R = SSM_HEADS // SSM_GROUPS
    sumsq = None

    @pl.when(pl.program_id(1) == 0)
    def _():
        state_ref[...] = jnp.zeros_like(state_ref)
        rhs_ref[...] = jnp.zeros_like(rhs_ref)

    li = lax.broadcasted_iota(jnp.int32, (L, L), 0)
    si = lax.broadcasted_iota(jnp.int32, (L, L), 1)
    causal = li >= si
    acs2 = acs_ref[0]
    acst2 = acst_ref[0]
    rowp_all = rowp_ref[0]

    bmat = b_ref[...].astype(F32)
    cmat = c_ref[...].astype(F32)
    cb = lax.dot_general(cmat.astype(BF16), bmat.astype(BF16), (((1,), (1,)), ((), ())),
                         preferred_element_type=F32)
    bt = bmat.T

    lane = lax.broadcasted_iota(jnp.int32, (L, LANES), 1)
    low = lane < P
    zeros_b = jnp.zeros((L, LANES), BF16)
    for tile in range(R // 4):
        for pair in range(2):
            cols = slice(tile * 256 + pair * LANES, tile * 256 + (pair + 1) * LANES)
            xb = x_ref[:, cols].astype(BF16)
            sb = state_ref[:, cols].astype(BF16)
            dst = slice(pair * LANES, (pair + 1) * LANES)
            for k, keep in enumerate((low, ~low)):
                base = (2 * pair + k) * L
                rhs_ref[tile, base:base + L, dst] = jnp.where(keep, xb, zeros_b)
                rhs_ref[tile, 4 * L + base:4 * L + base + L, dst] = jnp.where(keep, sb, zeros_b)

    head_of_lane = lax.broadcasted_iota(jnp.int32, (1, R * P), 1) // P
    state_decay = jnp.zeros((1, R * P), F32)
    lhs_y, lhs_state = [], []
    for tile in range(R // 4):
        on_x, on_state, to_state = [], [], []
        for r in range(4 * tile, 4 * tile + 4):
            col = acs2[:, r:r + 1]
            rowp = rowp_all[r:r + 1, :]
            last = acst2[r:r + 1, L - 1:L]
            on_x.append((cb * jnp.exp2(jnp.where(causal, col - rowp, -jnp.inf))).astype(BF16))
            on_state.append((cmat * jnp.exp2(col)).astype(BF16))
            to_state.append((bt * jnp.exp2(last - rowp)).astype(BF16))
            state_decay = jnp.where(head_of_lane == r, jnp.exp2(last), state_decay)
        lhs_y.append(jnp.concatenate(on_x + on_state, axis=1))
        lhs_state.append(jnp.concatenate(to_state, axis=1))
    y_mix = [jnp.dot(lhs_y[tile], rhs_ref[tile], preferred_element_type=F32)
             for tile in range(R // 4)]
    new_state = [jnp.dot(lhs_state[tile], rhs_ref[tile, 0:4 * L, :], preferred_element_type=F32)
                 for tile in range(R // 4)]
    for tile in range(R // 4):
        cols = slice(tile * 256, (tile + 1) * 256)
        y = y_mix[tile] + dskip_ref[0][:, cols] * x_ref[:, cols].astype(F32)
        gated = y * jax.nn.silu(z_ref[:, cols])
        gated_ref[:, cols] = gated
        part = jnp.sum(gated * gated, axis=-1, keepdims=True)
        sumsq = part if sumsq is None else sumsq + part
    for tile in range(R // 4):
        cols = slice(tile * 256, (tile + 1) * 256)
        state_ref[:, cols] = state_ref[:, cols] * state_decay[:, cols] + new_state[tile]
    scale = lax.rsqrt(sumsq * (1.0 / (R * P)) + EPS)
    y_ref[...] = (gated_ref[...] * scale * g_ref[...]).astype(y_ref.dtype)


def ssd(xbc, dt_raw, dt_bias, a_log, d_skip, z, norm_g):
    t = xbc.shape[0]
    G, R, L = SSM_GROUPS, SSM_HEADS // SSM_GROUPS, SSD_CHUNK
    gw = R * SSM_HEADDIM
    acs, acst, rowp = dt_prep(dt_raw, dt_bias, a_log)
    acs = acs.reshape(t, G, R).transpose(1, 0, 2)
    acst = acst.reshape(G, R, t)
    rowp = rowp.reshape(G, R, t)
    dskip = jnp.repeat(d_skip, SSM_HEADDIM).reshape(G, 1, gw)
    per_time = pl.BlockSpec((1, R, L), lambda g, c: (g, 0, c))
    nb = SSM_INNER // SSM_STATE
    return pl.pallas_call(
        _ssd_kernel,
        out_shape=jax.ShapeDtypeStruct((t, SSM_INNER), BF16),
        grid=(G, t // L),
        in_specs=[pl.BlockSpec((L, gw), lambda g, c: (c, g)),
                  pl.BlockSpec((L, SSM_STATE), lambda g, c: (c, nb + g)),
                  pl.BlockSpec((L, SSM_STATE), lambda g, c: (c, nb + G + g)),
                  pl.BlockSpec((1, L, R), lambda g, c: (g, c, 0)), per_time, per_time,
                  pl.BlockSpec((1, 1, gw), lambda g, c: (g, 0, 0)),
                  pl.BlockSpec((L, gw), lambda g, c: (c, g)),
                  pl.BlockSpec((1, gw), lambda g, c: (0, g))],
        out_specs=pl.BlockSpec((L, gw), lambda g, c: (c, g)),
        scratch_shapes=[pltpu.VMEM((SSM_STATE, gw), F32),
                        pltpu.VMEM((R // 4, 4 * 2 * L, 256), BF16),
                        pltpu.VMEM((L, gw), F32)],
        compiler_params=_params(("parallel", "arbitrary")),
        name="ssd",
    )(xbc, xbc, xbc, acs, acst, rowp, dskip, z, norm_g.reshape(1, SSM_INNER))


def kernel(x, mix_norm, ffn_norm, final_norm, ev_w_in, ev_conv_w, ev_sgu_norm, ev_sgu_w, ev_sgu_b, ev_w_out, od_w_in, od_conv_w, od_conv_b, od_dt_bias, od_a_log, od_d, od_norm, od_w_out, peer_wq, peer_keys, peer_u, peer_v):
    h = x[0]

    def peer(xt, i):
        return peer_ffn_t(xt, peer_wq[i].T.astype(BF16), peer_keys[i].astype(BF16),
                          peer_u[i].astype(BF16), peer_v[i].T.astype(BF16))

    hn = rmsnorm(h, mix_norm[0], transpose_out=False)
    proj = matmul(hn, ev_w_in[0].astype(BF16), bm=1024, bn=1024, name="ev_in")
    ycat = evmix(proj, ev_conv_w[0], ev_sgu_norm[0], ev_sgu_w[0], ev_sgu_b[0])
    h = matmul(ycat, ev_w_out[0].astype(BF16), bm=1024, bn=1024, res=h, name="ev_out")
    delta_t = peer(rmsnorm(h, ffn_norm[0], transpose_out=True), 0)

    h, hn = resid_norm(h, delta_t, mix_norm[1], want_h=True, norm_dtype=BF16)
    w_in = od_w_in[0]
    z = matmul(hn, w_in[:, :SSM_INNER].astype(BF16), bm=1024, bn=1024, name="od_in_z")
    xbc = matmul(hn, w_in[:, SSM_INNER:SSM_INNER + SSM_CONV_DIM].astype(BF16), bm=1024, bn=1024,
                 name="od_in_xbc")
    dt_raw = matmul(hn, w_in[:, SSM_INNER + SSM_CONV_DIM:].astype(BF16), bm=1024, bn=SSM_HEADS,
                    name="od_in_dt")
    xbc = conv_silu(xbc, od_conv_w[0], od_conv_b[0])
    yn = ssd(xbc, dt_raw, od_dt_bias[0], od_a_log[0], od_d[0], z, od_norm[0])
    h = matmul(yn, od_w_out[0].astype(BF16), bm=1024, bn=1024, bk=2048, res=h, name="od_out")
    delta_t = peer(rmsnorm(h, ffn_norm[1], transpose_out=True), 1)

    out = resid_norm(h, delta_t, final_norm, want_h=False, norm_dtype=F32)
    return out[None]
```

```python
import functools
import math

import jax
import jax.numpy as jnp
import numpy as np
from jax import lax
from jax.experimental import pallas as pl
from jax.experimental.pallas import tpu as pltpu

F32 = jnp.float32
BF16 = jnp.bfloat16

EPS = 1e-6
LANES = 128
BF16_ROWS = 16
VMEM_LIMIT = 56 * 1024 * 1024

D_MODEL = 4096
CONV_WIDTH = 2048
CONV_K = 3
SGU_WIDTH = 2048
SGU_HEADS = 16
SGU_BLOCK = 128
SSM_INNER = 8192
SSM_HEADDIM = 64
SSM_HEADS = 128
SSM_GROUPS = 8
SSM_STATE = 128
SSM_CONV_K = 4
SSM_CONV_DIM = SSM_INNER + 2 * SSM_GROUPS * SSM_STATE
SSD_CHUNK = 128
PEER_HEADS = 8
PEER_NKEYS = 128
PEER_HALF = 128
PEER_TOPK = 16
PEER_CAND = [(a, b) for a in range(PEER_TOPK) for b in range(PEER_TOPK)
             if (a + 1) * (b + 1) <= PEER_TOPK]
PEER_NCAND = len(PEER_CAND)
PEER_NCAND_PAD = -(-PEER_NCAND // 8) * 8


def _params(sem):
    return pltpu.CompilerParams(dimension_semantics=sem, vmem_limit_bytes=VMEM_LIMIT)


def _mm_kernel(*refs, nk, has_res):
    if has_res:
        a_ref, b_ref, r_ref, o_ref = refs
    else:
        a_ref, b_ref, o_ref = refs
        r_ref = None

    if nk == 1:
        part = jnp.dot(a_ref[...], b_ref[...], preferred_element_type=F32)
        if has_res:
            part = part + r_ref[...]
        o_ref[...] = part.astype(o_ref.dtype)
    else:
        @pl.when(pl.program_id(2) == 0)
        def _():
            o_ref[...] = r_ref[...] if has_res else jnp.zeros_like(o_ref)

        o_ref[...] += jnp.dot(a_ref[...], b_ref[...], preferred_element_type=F32)


def matmul(a, b, *, bm, bn, bk=None, out_dtype=F32, res=None, name="mm"):
    m, kdim = a.shape
    _, n = b.shape
    bk = kdim if bk is None else bk
    bm, bn = min(bm, m), min(bn, n)
    nk = kdim // bk
    assert m % bm == 0 and n % bn == 0 and kdim % bk == 0
    assert nk == 1 or out_dtype == F32
    in_specs = [pl.BlockSpec((bm, bk), lambda i, j, k: (i, k)),
                pl.BlockSpec((bk, bn), lambda i, j, k: (k, j))]
    args = [a, b]
    if res is not None:
        in_specs.append(pl.BlockSpec((bm, bn), lambda i, j, k: (i, j)))
        args.append(res)
    return pl.pallas_call(
        functools.partial(_mm_kernel, nk=nk, has_res=res is not None),
        out_shape=jax.ShapeDtypeStruct((m, n), out_dtype),
        grid=(m // bm, n // bn, nk),
        in_specs=in_specs,
        out_specs=pl.BlockSpec((bm, bn), lambda i, j, k: (i, j)),
        compiler_params=_params(("parallel", "parallel", "arbitrary")),
        name=name,
    )(*args)


def _mm_wcast_kernel(a_ref, w_ref, o_ref, wb_ref):
    @pl.when(pl.program_id(1) == 0)
    def _():
        wb_ref[...] = w_ref[...].astype(BF16)

    o_ref[...] = jnp.dot(a_ref[...], wb_ref[...], preferred_element_type=F32).astype(o_ref.dtype)


def matmul_wcast(a, w, col0, ncols, *, bm, bn, out_dtype=F32, name="mm_wcast"):
    m, kdim = a.shape
    bm, bn = min(bm, m), min(bn, ncols)
    assert m % bm == 0 and ncols % bn == 0 and col0 % bn == 0 and w.shape[0] == kdim
    return pl.pallas_call(
        _mm_wcast_kernel,
        out_shape=jax.ShapeDtypeStruct((m, ncols), out_dtype),
        grid=(ncols // bn, m // bm),
        in_specs=[pl.BlockSpec((bm, kdim), lambda j, i: (i, 0)),
                  pl.BlockSpec((kdim, bn), lambda j, i: (0, col0 // bn + j),
                               pipeline_mode=pl.Buffered(1))],
        out_specs=pl.BlockSpec((bm, bn), lambda j, i: (i, j)),
        scratch_shapes=[pltpu.VMEM((kdim, bn), BF16)],
        compiler_params=_params(("parallel", "arbitrary")),
        name=name,
    )(a, w)


_GELU_K0 = -2.0 * math.sqrt(2.0 / math.pi) * math.log2(math.e)
_GELU_K1 = _GELU_K0 * 0.044715


def _gelu_tanh(x):
    return x / (1.0 + jnp.exp2(x * (_GELU_K0 + _GELU_K1 * (x * x))))


def _rms(x, g):
    return x * lax.rsqrt(jnp.mean(x * x, axis=-1, keepdims=True) + EPS) * g


def _rmsnorm_kernel(x_ref, g_ref, o_ref, *, transpose_out):
    y = _rms(x_ref[...], g_ref[...])
    if transpose_out:
        y = y.T
    o_ref[...] = y.astype(o_ref.dtype)


def rmsnorm(x, g, *, transpose_out, bt=256):
    t, d = x.shape
    if transpose_out:
        out_shape, out_spec = (d, t), pl.BlockSpec((d, bt), lambda i: (0, i))
    else:
        out_shape, out_spec = (t, d), pl.BlockSpec((bt, d), lambda i: (i, 0))
    return pl.pallas_call(
        functools.partial(_rmsnorm_kernel, transpose_out=transpose_out),
        out_shape=jax.ShapeDtypeStruct(out_shape, BF16),
        grid=(t // bt,),
        in_specs=[pl.BlockSpec((bt, d), lambda i: (i, 0)),
                  pl.BlockSpec((1, d), lambda i: (0, 0))],
        out_specs=out_spec,
        compiler_params=_params(("parallel",)),
        name="rmsnorm_t" if transpose_out else "rmsnorm",
    )(x, g.reshape(1, d))


def _resid_norm_kernel(h_ref, dt_ref, g_ref, *o_refs, want_h):
    h = h_ref[...] + dt_ref[...].T
    if want_h:
        hn_ref, n_ref = o_refs
        hn_ref[...] = h
    else:
        n_ref, = o_refs
    n_ref[...] = _rms(h, g_ref[...]).astype(n_ref.dtype)


def resid_norm(h, delta_t, g, *, want_h, norm_dtype, bt=256):
    t, d = h.shape
    row = pl.BlockSpec((bt, d), lambda i: (i, 0))
    norm_shape = jax.ShapeDtypeStruct((t, d), norm_dtype)
    if want_h:
        out_shape, out_specs = (jax.ShapeDtypeStruct((t, d), F32), norm_shape), (row, row)
    else:
        out_shape, out_specs = norm_shape, row
    return pl.pallas_call(
        functools.partial(_resid_norm_kernel, want_h=want_h),
        out_shape=out_shape,
        grid=(t // bt,),
        in_specs=[row, pl.BlockSpec((d, bt), lambda i: (0, i)),
                  pl.BlockSpec((1, d), lambda i: (0, 0))],
        out_specs=out_specs,
        compiler_params=_params(("parallel",)),
        name="resid_norm",
    )(h, delta_t, g.reshape(1, d))


def _shift_rows(p, prev, k):
    rolled = pltpu.roll(p, k, axis=0)
    head = pltpu.roll(prev, k, axis=0)
    rows = lax.broadcasted_iota(jnp.int32, (8, p.shape[1]), 0)
    fixed = jnp.where(rows < k, head, rolled[:8])
    return jnp.concatenate([fixed, rolled[8:]], axis=0)


def _evmix_kernel(proj_ref, cw_ref, ng_ref, sw_ref, sbt_ref, o_ref, carry_ref):
    @pl.when(pl.program_id(0) == 0)
    def _():
        carry_ref[...] = jnp.zeros_like(carry_ref)

    c = CONV_WIDTH
    gb = proj_ref[:, 0:c]
    p = proj_ref[:, c:2 * c] * proj_ref[:, 2 * c:3 * c]
    prev = carry_ref[...]
    conv = p * cw_ref[CONV_K - 1:CONV_K, :]
    for k in range(1, CONV_K):
        conv = conv + _shift_rows(p, prev, k) * cw_ref[CONV_K - 1 - k:CONV_K - k, :]
    carry_ref[...] = p[p.shape[0] - 8:]
    o_ref[:, 0:c] = (gb * conv).astype(o_ref.dtype)

    u = _gelu_tanh(proj_ref[:, 3 * c:3 * c + SGU_WIDTH])
    v = _gelu_tanh(proj_ref[:, 3 * c + SGU_WIDTH:3 * c + 2 * SGU_WIDTH])
    mu = jnp.mean(v, axis=-1, keepdims=True)
    vc = v - mu
    var = jnp.mean(vc * vc, axis=-1, keepdims=True)
    vn = (vc * lax.rsqrt(var + EPS) * ng_ref[...]).astype(BF16)
    q_idx = lax.broadcasted_iota(jnp.int32, (SGU_BLOCK, SGU_BLOCK), 0)
    p_idx = lax.broadcasted_iota(jnp.int32, (SGU_BLOCK, SGU_BLOCK), 1)
    hw = SGU_WIDTH // SGU_HEADS
    for g in range(SGU_HEADS):
        ws = jnp.where(q_idx >= p_idx, sw_ref[g], 0.0).astype(BF16)
        sv = jnp.dot(ws, vn[:, g * hw:(g + 1) * hw], preferred_element_type=F32)
        sv = sv + sbt_ref[:, g:g + 1]
        o_ref[:, c + g * hw:c + (g + 1) * hw] = (u[:, g * hw:(g + 1) * hw] * sv).astype(o_ref.dtype)


def evmix(proj, conv_w, sgu_norm, sgu_w, sgu_b):
    t, width = proj.shape
    bt = SGU_BLOCK
    return pl.pallas_call(
        _evmix_kernel,
        out_shape=jax.ShapeDtypeStruct((t, CONV_WIDTH + SGU_WIDTH), BF16),
        grid=(t // bt,),
        in_specs=[pl.BlockSpec((bt, width), lambda i: (i, 0)),
                  pl.BlockSpec((CONV_K, CONV_WIDTH), lambda i: (0, 0)),
                  pl.BlockSpec((1, SGU_WIDTH), lambda i: (0, 0)),
                  pl.BlockSpec((SGU_HEADS, SGU_BLOCK, SGU_BLOCK), lambda i: (0, 0, 0)),
                  pl.BlockSpec((SGU_BLOCK, SGU_HEADS), lambda i: (0, 0))],
        out_specs=pl.BlockSpec((bt, CONV_WIDTH + SGU_WIDTH), lambda i: (i, 0)),
        scratch_shapes=[pltpu.VMEM((8, CONV_WIDTH), F32)],
        compiler_params=_params(("arbitrary",)),
        name="evmix",
    )(proj, conv_w, sgu_norm.reshape(1, SGU_WIDTH), sgu_w, jnp.transpose(sgu_b))


def _top16(s):
    n, width = s.shape
    rows = lax.broadcasted_iota(jnp.int32, s.shape, 0).astype(F32)
    slot = lax.broadcasted_iota(jnp.int32, (PEER_TOPK, width), 0)
    cur = s
    rank = jnp.full(s.shape, float(PEER_TOPK), F32)
    vals = jnp.zeros((PEER_TOPK, width), F32)
    for r in range(PEER_TOPK):
        m = jnp.max(cur, axis=0, keepdims=True)
        first = jnp.min(jnp.where(cur == m, rows, float(n)), axis=0, keepdims=True)
        sel = rows == first
        rank = jnp.where(sel, float(r), rank)
        cur = jnp.where(sel, -jnp.inf, cur)
        vals = jnp.where(slot == r, m, vals)
    return rank, vals


def _top16_distinct(s):
    ax = s.ndim - 2
    slot = lax.broadcasted_iota(jnp.int32, s.shape[:ax] + (PEER_TOPK, s.shape[-1]), ax)
    cur = s
    vals = jnp.zeros(slot.shape, F32)
    for r in range(PEER_TOPK):
        m = jnp.max(cur, axis=ax, keepdims=True)
        cur = jnp.where(cur == m, -jnp.inf, cur)
        vals = jnp.where(slot == r, m, vals)
    last = vals[..., PEER_TOPK - 1:PEER_TOPK, :]
    count = jnp.sum(jnp.where(s >= last, 1.0, 0.0), axis=ax, keepdims=True)
    return vals, count


def _route_kernel(qt_ref, keys_ref, seta_ref, setb_ref, rowsel_ref,
                  lrow_ref, e1_ref, rank2_ref, e2_ref,
                  s_ref, t_ref, n_ref, cand_ref, tc_ref, nc_ref):
    H = PEER_HEADS
    k = float(PEER_TOPK)

    def candidates(v1, v2):
        cand = (jnp.dot(seta_ref[...], v1, preferred_element_type=F32, precision=lax.Precision.HIGHEST)
                + jnp.dot(setb_ref[...], v2, preferred_element_type=F32, precision=lax.Precision.HIGHEST))
        crow = lax.broadcasted_iota(jnp.int32, cand.shape, 0)
        return jnp.where(crow < PEER_NCAND, cand, -jnp.inf)

    for h in range(H):
        for side in range(2):
            q = qt_ref[(2 * h + side) * PEER_HALF:(2 * h + side + 1) * PEER_HALF, :].astype(BF16)
            s_ref[side * H + h] = jnp.dot(keys_ref[h, side], q, preferred_element_type=F32)
    t_all, n_all = _top16_distinct(s_ref[...])
    t_ref[...] = t_all
    n_ref[...] = n_all
    for h in range(H):
        cand_ref[h] = candidates(t_ref[h], t_ref[H + h])
    tc_all, nc_all = _top16_distinct(cand_ref[...])
    tc_ref[...] = tc_all
    nc_ref[...] = nc_all

    def head(h, carry):
        s1 = s_ref[h]
        s2 = s_ref[H + h]

        def finish(cand, chosen, best, lrow_of, rank2):
            z = jnp.sum(jnp.where(chosen, jnp.exp(cand - best), 0.0), axis=0, keepdims=True)
            length = jnp.dot(rowsel_ref[...], chosen.astype(F32), preferred_element_type=F32)
            lrow = jnp.zeros_like(s1)
            for a in range(PEER_TOPK):
                lrow = jnp.where(lrow_of(a), length[a:a + 1, :], lrow)
            lrow_ref[h] = lrow
            rank2_ref[h] = rank2
            e2_ref[h] = jnp.exp(s2 - jnp.max(s2, axis=0, keepdims=True)) / z

        e1_ref[h] = jnp.exp(s1 - jnp.max(s1, axis=0, keepdims=True))
        ties = jnp.max(jnp.abs(n_ref[h] - k) + jnp.abs(n_ref[H + h] - k) + jnp.abs(nc_ref[h] - k))

        @pl.when(ties == 0.0)
        def _():
            t1 = t_ref[h]
            t2 = t_ref[H + h]
            tc = tc_ref[h]
            cand = cand_ref[h]
            rank2 = jnp.full(s2.shape, k, F32)
            for r in range(PEER_TOPK):
                rank2 = jnp.where(s2 == t2[r:r + 1, :], float(r), rank2)
            finish(cand, cand >= tc[PEER_TOPK - 1:PEER_TOPK, :], tc[0:1, :],
                   lambda a: s1 == t1[a:a + 1, :], rank2)

        @pl.when(ties != 0.0)
        def _():
            rank1, v1 = _top16(s1)
            rank2, v2 = _top16(s2)
            cand_exact = candidates(v1, v2)
            crank, cvals = _top16(cand_exact)
            finish(cand_exact, crank < k, cvals[0:1, :], lambda a: rank1 == float(a), rank2)

        return carry

    any_ties = jnp.max(jnp.abs(n_all[:H] - k) + jnp.abs(n_all[H:] - k) + jnp.abs(nc_all - k))

    @pl.when(any_ties == 0.0)
    def _():
        s1 = s_ref[0:H]
        s2 = s_ref[H:2 * H]
        t1 = t_ref[0:H]
        t2 = t_ref[H:2 * H]
        tc = tc_ref[...]
        cand = cand_ref[...]
        chosen = cand >= tc[:, PEER_TOPK - 1:PEER_TOPK, :]
        z = jnp.sum(jnp.where(chosen, jnp.exp(cand - tc[:, 0:1, :]), 0.0), axis=1, keepdims=True)
        chosen_f = chosen.astype(F32)
        length = jnp.stack([jnp.dot(rowsel_ref[...], chosen_f[h], preferred_element_type=F32)
                            for h in range(H)], axis=0)
        lrow = jnp.zeros_like(s1)
        rank2 = jnp.full(s2.shape, k, F32)
        for a in range(PEER_TOPK):
            lrow = jnp.where(s1 == t1[:, a:a + 1, :], length[:, a:a + 1, :], lrow)
            rank2 = jnp.where(s2 == t2[:, a:a + 1, :], float(a), rank2)
        lrow_ref[...] = lrow
        rank2_ref[...] = rank2
        e1_ref[...] = jnp.exp(s1 - t1[:, 0:1, :])
        e2_ref[...] = jnp.exp(s2 - t2[:, 0:1, :]) / z

    @pl.when(any_ties != 0.0)
    def _():
        lax.fori_loop(0, PEER_HEADS, head, 0)


def peer_route(qt, keys_bf16):
    _, t = qt.shape
    bt = LANES
    seta = np.zeros((PEER_NCAND_PAD, PEER_TOPK), np.float32)
    setb = np.zeros((PEER_NCAND_PAD, PEER_TOPK), np.float32)
    rowsel = np.zeros((PEER_TOPK, PEER_NCAND_PAD), np.float32)
    for k, (a, b) in enumerate(PEER_CAND):
        seta[k, a] = 1.0
        setb[k, b] = 1.0
        rowsel[a, k] = 1.0
    table = jax.ShapeDtypeStruct((PEER_HEADS, PEER_NKEYS, t), F32)
    table16 = jax.ShapeDtypeStruct((PEER_HEADS, PEER_NKEYS, t), BF16)
    tspec = pl.BlockSpec((PEER_HEADS, PEER_NKEYS, bt), lambda i: (0, 0, i))
    const2 = lambda i: (0, 0)
    return pl.pallas_call(
        _route_kernel,
        out_shape=(table, table, table, table),
        grid=(t // bt,),
        in_specs=[pl.BlockSpec((PEER_HEADS * 2 * PEER_HALF, bt), lambda i: (0, i)),
                  pl.BlockSpec((PEER_HEADS, 2, PEER_NKEYS, PEER_HALF), lambda i: (0, 0, 0, 0)),
                  pl.BlockSpec((PEER_NCAND_PAD, PEER_TOPK), const2),
                  pl.BlockSpec((PEER_NCAND_PAD, PEER_TOPK), const2),
                  pl.BlockSpec((PEER_TOPK, PEER_NCAND_PAD), const2)],
        out_specs=(tspec, tspec, tspec, tspec),
        scratch_shapes=[pltpu.VMEM((2 * PEER_HEADS, PEER_NKEYS, bt), F32),
                        pltpu.VMEM((2 * PEER_HEADS, PEER_TOPK, bt), F32),
                        pltpu.VMEM((2 * PEER_HEADS, 1, bt), F32),
                        pltpu.VMEM((PEER_HEADS, PEER_NCAND_PAD, bt), F32),
                        pltpu.VMEM((PEER_HEADS, PEER_TOPK, bt), F32),
                        pltpu.VMEM((PEER_HEADS, 1, bt), F32)],
        compiler_params=_params(("parallel",)),
        name="peer_route",
    )(qt, keys_bf16, seta, setb, rowsel)


def _peer_act_kernel(u_ref, xt_ref, o_ref):
    hid = jnp.dot(u_ref[...], xt_ref[...], preferred_element_type=F32)
    o_ref[...] = _gelu_tanh(hid).astype(o_ref.dtype)


def peer_act(u_bf16, xt, *, bt=512, be=2048):
    e, d = u_bf16.shape
    _, t = xt.shape
    return pl.pallas_call(
        _peer_act_kernel,
        out_shape=jax.ShapeDtypeStruct((e, t), BF16),
        grid=(t // bt, e // be),
        in_specs=[pl.BlockSpec((be, d), lambda i, j: (j, 0)),
                  pl.BlockSpec((d, bt), lambda i, j: (0, i))],
        out_specs=pl.BlockSpec((be, bt), lambda i, j: (j, i)),
        compiler_params=_params(("parallel", "arbitrary")),
        name="peer_act",
    )(u_bf16, xt)


def _peer_out_kernel(vt_ref, a_ref, lrow_ref, e1_ref, rank2_ref, e2_ref, o_ref, p_ref, *, nsub):
    e = pl.program_id(1)
    bt = o_ref.shape[1]
    nparts = PEER_NKEYS // BF16_ROWS

    @pl.when(e == 0)
    def _():
        o_ref[...] = jnp.zeros_like(o_ref)

    first_rows = [[(lrow_ref[h, pl.ds(e * nsub + ii, 1), :], e1_ref[h, pl.ds(e * nsub + ii, 1), :])
                   for ii in range(nsub)] for h in range(PEER_HEADS)]
    for tb in range(bt // LANES):
        cols = slice(tb * LANES, (tb + 1) * LANES)
        acc = [[None] * nparts for _ in range(nsub)]
        for h in range(PEER_HEADS):
            bcast = lambda row: jnp.broadcast_to(row[:, cols], (BF16_ROWS, LANES)).astype(BF16)
            length = [bcast(first_rows[h][ii][0]) for ii in range(nsub)]
            gate1 = [bcast(first_rows[h][ii][1]) for ii in range(nsub)]
            for p in range(nparts):
                rows = slice(p * BF16_ROWS, (p + 1) * BF16_ROWS)
                rank2 = rank2_ref[h, rows, cols].astype(BF16)
                gate2 = e2_ref[h, rows, cols].astype(BF16)
                for ii in range(nsub):
                    contrib = jnp.where(rank2 < length[ii], gate2, jnp.zeros_like(gate2)) * gate1[ii]
                    acc[ii][p] = contrib if acc[ii][p] is None else acc[ii][p] + contrib
        for ii in range(nsub):
            for p in range(nparts):
                rows = slice(ii * PEER_NKEYS + p * BF16_ROWS, ii * PEER_NKEYS + (p + 1) * BF16_ROWS)
                p_ref[rows, cols] = acc[ii][p] * a_ref[rows, cols]
    o_ref[...] += jnp.dot(vt_ref[...], p_ref[...], preferred_element_type=F32)


def peer_out(v_t, at, tables, *, bt=512, be=512):
    d, e = v_t.shape
    _, t = at.shape
    tspec = pl.BlockSpec((PEER_HEADS, PEER_NKEYS, bt), lambda i, j: (0, 0, i))
    return pl.pallas_call(
        functools.partial(_peer_out_kernel, nsub=be // PEER_NKEYS),
        out_shape=jax.ShapeDtypeStruct((d, t), F32),
        grid=(t // bt, e // be),
        in_specs=[pl.BlockSpec((d, be), lambda i, j: (0, j)),
                  pl.BlockSpec((be, bt), lambda i, j: (j, i)),
                  tspec, tspec, tspec, tspec],
        out_specs=pl.BlockSpec((d, bt), lambda i, j: (0, i)),
        scratch_shapes=[pltpu.VMEM((be, bt), BF16)],
        compiler_params=_params(("parallel", "arbitrary")),
        name="peer_out",
    )(v_t, at, *tables)


def peer_ffn_t(xt, wq_t, keys_bf16, u_bf16, v_t):
    qt = matmul(wq_t, xt, bm=1024, bn=1024, name="peer_q")
    tables = peer_route(qt, keys_bf16)
    return peer_out(v_t, peer_act(u_bf16, xt), tables)


def _conv_silu_kernel(x_ref, w_ref, b_ref, o_ref, carry_ref):
    @pl.when(pl.program_id(1) == 0)
    def _():
        carry_ref[...] = jnp.zeros_like(carry_ref)

    x = x_ref[...]
    prev = carry_ref[...]
    acc = x * w_ref[SSM_CONV_K - 1:SSM_CONV_K, :] + b_ref[...]
    for k in range(1, SSM_CONV_K):
        acc = acc + _shift_rows(x, prev, k) * w_ref[SSM_CONV_K - 1 - k:SSM_CONV_K - k, :]
    carry_ref[...] = x[x.shape[0] - 8:]
    o_ref[...] = jax.nn.silu(acc).astype(o_ref.dtype)


def conv_silu(x, w, b, *, bt=512, bc=1024):
    t, c = x.shape
    return pl.pallas_call(
        _conv_silu_kernel,
        out_shape=jax.ShapeDtypeStruct((t, c), BF16),
        grid=(c // bc, t // bt),
        in_specs=[pl.BlockSpec((bt, bc), lambda j, i: (i, j)),
                  pl.BlockSpec((SSM_CONV_K, bc), lambda j, i: (0, j)),
                  pl.BlockSpec((1, bc), lambda j, i: (0, j))],
        out_specs=pl.BlockSpec((bt, bc), lambda j, i: (i, j)),
        scratch_shapes=[pltpu.VMEM((8, bc), F32)],
        compiler_params=_params(("parallel", "arbitrary")),
        name="conv_silu",
    )(x, w, b.reshape(1, c))


def _dt_prep_kernel(dt_ref, bias_ref, alog_ref, acs_ref, acst_ref, rowp_ref):
    L = SSD_CHUNK
    dt = jax.nn.softplus(dt_ref[...] + bias_ref[...])
    da = dt * (-jnp.exp(alog_ref[...]))
    li = lax.broadcasted_iota(jnp.int32, (L, L), 0)
    si = lax.broadcasted_iota(jnp.int32, (L, L), 1)
    acs = jnp.dot((li >= si).astype(F32), da, preferred_element_type=F32,
                  precision=lax.Precision.HIGHEST) * math.log2(math.e)
    acs_ref[...] = acs
    acst_ref[...] = acs.T
    rowp_ref[...] = (acs - jnp.log2(dt)).T


def dt_prep(dt_raw, dt_bias, a_log):
    t, heads = dt_raw.shape
    L = SSD_CHUNK
    vec = pl.BlockSpec((1, heads), lambda c: (0, 0))
    tr = pl.BlockSpec((heads, L), lambda c: (0, c))
    return pl.pallas_call(
        _dt_prep_kernel,
        out_shape=(jax.ShapeDtypeStruct((t, heads), F32), jax.ShapeDtypeStruct((heads, t), F32),
                   jax.ShapeDtypeStruct((heads, t), F32)),
        grid=(t // L,),
        in_specs=[pl.BlockSpec((L, heads), lambda c: (c, 0)), vec, vec],
        out_specs=(pl.BlockSpec((L, heads), lambda c: (c, 0)), tr, tr),
        compiler_params=_params(("parallel",)),
        name="dt_prep",
    )(dt_raw, dt_bias.reshape(1, heads), a_log.reshape(1, heads))


def _ssd_kernel(x_ref, b_ref, c_ref, acs_ref, acst_ref, rowp_ref,
                dskip_ref, z_ref, g_ref, y_ref, state_ref, rhs_ref, gated_ref):
    L = SSD_CHUNK
    P = SSM_HEADDIM
    R = SSM_HEADS // SSM_GROUPS
    sumsq = None

    @pl.when(pl.program_id(1) == 0)
    def _():
        state_ref[...] = jnp.zeros_like(state_ref)
        rhs_ref[...] = jnp.zeros_like(rhs_ref)

    li = lax.broadcasted_iota(jnp.int32, (L, L), 0)
    si = lax.broadcasted_iota(jnp.int32, (L, L), 1)
    causal = li >= si
    acs2 = acs_ref[0]
    acst2 = acst_ref[0]
    rowp_all = rowp_ref[0]

    bmat = b_ref[...].astype(F32)
    cmat = c_ref[...].astype(F32)
    cb = lax.dot_general(cmat.astype(BF16), bmat.astype(BF16), (((1,), (1,)), ((), ())),
                         preferred_element_type=F32)
    bt = bmat.T

    lane = lax.broadcasted_iota(jnp.int32, (L, LANES), 1)
    low = lane < P
    zeros_b = jnp.zeros((L, LANES), BF16)
    for tile in range(R // 4):
        for pair in range(2):
            cols = slice(tile * 256 + pair * LANES, tile * 256 + (pair + 1) * LANES)
            xb = x_ref[:, cols].astype(BF16)
            sb = state_ref[:, cols].astype(BF16)
            dst = slice(pair * LANES, (pair + 1) * LANES)
            for k, keep in enumerate((low, ~low)):
                base = (2 * pair + k) * L
                rhs_ref[tile, base:base + L, dst] = jnp.where(keep, xb, zeros_b)
                rhs_ref[tile, 4 * L + base:4 * L + base + L, dst] = jnp.where(keep, sb, zeros_b)

    head_of_lane = lax.broadcasted_iota(jnp.int32, (1, R * P), 1) // P
    state_decay = jnp.zeros((1, R * P), F32)
    lhs_y, lhs_state = [], []
    for tile in range(R // 4):
        on_x, on_state, to_state = [], [], []
        for r in range(4 * tile, 4 * tile + 4):
            col = acs2[:, r:r + 1]
            rowp = rowp_all[r:r + 1, :]
            last = acst2[r:r + 1, L - 1:L]
            on_x.append((cb * jnp.exp2(jnp.where(causal, col - rowp, -jnp.inf))).astype(BF16))
            on_state.append((cmat * jnp.exp2(col)).astype(BF16))
            to_state.append((bt * jnp.exp2(last - rowp)).astype(BF16))
            state_decay = jnp.where(head_of_lane == r, jnp.exp2(last), state_decay)
        lhs_y.append(jnp.concatenate(on_x + on_state, axis=1))
        lhs_state.append(jnp.concatenate(to_state, axis=1))
    y_mix = [jnp.dot(lhs_y[tile], rhs_ref[tile], preferred_element_type=F32)
             for tile in range(R // 4)]
    new_state = [jnp.dot(lhs_state[tile], rhs_ref[tile, 0:4 * L, :], preferred_element_type=F32)
                 for tile in range(R // 4)]
    for tile in range(R // 4):
        cols = slice(tile * 256, (tile + 1) * 256)
        y = y_mix[tile] + dskip_ref[0][:, cols] * x_ref[:, cols].astype(F32)
        gated = y * jax.nn.silu(z_ref[:, cols])
        gated_ref[:, cols] = gated
        part = jnp.sum(gated * gated, axis=-1, keepdims=True)
        sumsq = part if sumsq is None else sumsq + part
    for tile in range(R // 4):
        cols = slice(tile * 256, (tile + 1) * 256)
        state_ref[:, cols] = state_ref[:, cols] * state_decay[:, cols] + new_state[tile]
    scale = lax.rsqrt(sumsq * (1.0 / (R * P)) + EPS)
    y_ref[...] = (gated_ref[...] * scale * g_ref[...]).astype(y_ref.dtype)


def ssd(xbc, dt_raw, dt_bias, a_log, d_skip, z, norm_g):
    t = xbc.shape[0]
    G, R, L = SSM_GROUPS, SSM_HEADS // SSM_GROUPS, SSD_CHUNK
    gw = R * SSM_HEADDIM
    acs, acst, rowp = dt_prep(dt_raw, dt_bias, a_log)
    acs = acs.reshape(t, G, R).transpose(1, 0, 2)
    acst = acst.reshape(G, R, t)
    rowp = rowp.reshape(G, R, t)
    dskip = jnp.repeat(d_skip, SSM_HEADDIM).reshape(G, 1, gw)
    per_time = pl.BlockSpec((1, R, L), lambda g, c: (g, 0, c))
    nb = SSM_INNER // SSM_STATE
    return pl.pallas_call(
        _ssd_kernel,
        out_shape=jax.ShapeDtypeStruct((t, SSM_INNER), BF16),
        grid=(G, t // L),
        in_specs=[pl.BlockSpec((L, gw), lambda g, c: (c, g)),
                  pl.BlockSpec((L, SSM_STATE), lambda g, c: (c, nb + g)),
                  pl.BlockSpec((L, SSM_STATE), lambda g, c: (c, nb + G + g)),
                  pl.BlockSpec((1, L, R), lambda g, c: (g, c, 0)), per_time, per_time,
                  pl.BlockSpec((1, 1, gw), lambda g, c: (g, 0, 0)),
                  pl.BlockSpec((L, gw), lambda g, c: (c, g)),
                  pl.BlockSpec((1, gw), lambda g, c: (0, g))],
        out_specs=pl.BlockSpec((L, gw), lambda g, c: (c, g)),
        scratch_shapes=[pltpu.VMEM((SSM_STATE, gw), F32),
                        pltpu.VMEM((R // 4, 4 * 2 * L, 256), BF16),
                        pltpu.VMEM((L, gw), F32)],
        compiler_params=_params(("parallel", "arbitrary")),
        name="ssd",
    )(xbc, xbc, xbc, acs, acst, rowp, dskip, z, norm_g.reshape(1, SSM_INNER))


def kernel(x, mix_norm, ffn_norm, final_norm, ev_w_in, ev_conv_w, ev_sgu_norm, ev_sgu_w, ev_sgu_b, ev_w_out, od_w_in, od_conv_w, od_conv_b, od_dt_bias, od_a_log, od_d, od_norm, od_w_out, peer_wq, peer_keys, peer_u, peer_v):
    h = x[0]

    def peer(xt, i):
        return peer_ffn_t(xt, peer_wq[i].T.astype(BF16), peer_keys[i].astype(BF16),
                          peer_u[i].astype(BF16), peer_v[i].T.astype(BF16))

    hn = rmsnorm(h, mix_norm[0], transpose_out=False)
    proj = matmul_wcast(hn, ev_w_in[0], 0, ev_w_in.shape[2], bm=1024, bn=1024, name="ev_in")
    ycat = evmix(proj, ev_conv_w[0], ev_sgu_norm[0], ev_sgu_w[0], ev_sgu_b[0])
    h = matmul(ycat, ev_w_out[0].astype(BF16), bm=1024, bn=1024, res=h, name="ev_out")
    delta_t = peer(rmsnorm(h, ffn_norm[0], transpose_out=True), 0)

    h, hn = resid_norm(h, delta_t, mix_norm[1], want_h=True, norm_dtype=BF16)
    w_in = od_w_in[0]
    z = matmul_wcast(hn, w_in, 0, SSM_INNER, bm=1024, bn=1024, name="od_in_z")
    xbc = matmul_wcast(hn, w_in, SSM_INNER, SSM_CONV_DIM, bm=1024, bn=1024, name="od_in_xbc")
    dt_raw = matmul_wcast(hn, w_in, SSM_INNER + SSM_CONV_DIM, SSM_HEADS, bm=1024, bn=SSM_HEADS,
                          name="od_in_dt")
    xbc = conv_silu(xbc, od_conv_w[0], od_conv_b[0])
    yn = ssd(xbc, dt_raw, od_dt_bias[0], od_a_log[0], od_d[0], z, od_norm[0])
    h = matmul(yn, od_w_out[0].astype(BF16), bm=1024, bn=1024, bk=2048, res=h, name="od_out")
    delta_t = peer(rmsnorm(h, ffn_norm[1], transpose_out=True), 1)

    out = resid_norm(h, delta_t, final_norm, want_h=False, norm_dtype=F32)
    return out[None]
```

```python
import functools
import math

import jax
import jax.numpy as jnp
import numpy as np
from jax import lax
from jax.experimental import pallas as pl
from jax.experimental.pallas import tpu as pltpu

F32 = jnp.float32
BF16 = jnp.bfloat16

EPS = 1e-6
LANES = 128
BF16_ROWS = 16
VMEM_LIMIT = 56 * 1024 * 1024

D_MODEL = 4096
CONV_WIDTH = 2048
CONV_K = 3
SGU_WIDTH = 2048
SGU_HEADS = 16
SGU_BLOCK = 128
SSM_INNER = 8192
SSM_HEADDIM = 64
SSM_HEADS = 128
SSM_GROUPS = 8
SSM_STATE = 128
SSM_CONV_K = 4
SSM_CONV_DIM = SSM_INNER + 2 * SSM_GROUPS * SSM_STATE
SSD_CHUNK = 128
PEER_HEADS = 8
PEER_NKEYS = 128
PEER_HALF = 128
PEER_TOPK = 16
PEER_CAND = [(a, b) for a in range(PEER_TOPK) for b in range(PEER_TOPK)
             if (a + 1) * (b + 1) <= PEER_TOPK]
PEER_NCAND = len(PEER_CAND)
PEER_NCAND_PAD = -(-PEER_NCAND // 8) * 8
PEER_SLAB_GROUP = 4


def _spec2d(arr, block, index_map, **kwargs):
    if arr.ndim == 2:
        return pl.BlockSpec(block, index_map, **kwargs)
    assert arr.ndim == 3 and arr.shape[0] == 1
    return pl.BlockSpec((None,) + tuple(block), lambda *idx: (0,) + tuple(index_map(*idx)), **kwargs)


def _params(sem):
    return pltpu.CompilerParams(dimension_semantics=sem, vmem_limit_bytes=VMEM_LIMIT)


def _mm_kernel(*refs, nk, has_res):
    if has_res:
        a_ref, b_ref, r_ref, o_ref = refs
    else:
        a_ref, b_ref, o_ref = refs
        r_ref = None

    if nk == 1:
        part = jnp.dot(a_ref[...], b_ref[...], preferred_element_type=F32)
        if has_res:
            part = part + r_ref[...]
        o_ref[...] = part.astype(o_ref.dtype)
    else:
        @pl.when(pl.program_id(2) == 0)
        def _():
            o_ref[...] = r_ref[...] if has_res else jnp.zeros_like(o_ref)

        o_ref[...] += jnp.dot(a_ref[...], b_ref[...], preferred_element_type=F32)


def matmul(a, b, *, bm, bn, bk=None, out_dtype=F32, res=None, name="mm"):
    m, kdim = a.shape
    _, n = b.shape
    bk = kdim if bk is None else bk
    bm, bn = min(bm, m), min(bn, n)
    nk = kdim // bk
    assert m % bm == 0 and n % bn == 0 and kdim % bk == 0
    assert nk == 1 or out_dtype == F32
    in_specs = [pl.BlockSpec((bm, bk), lambda i, j, k: (i, k)),
                pl.BlockSpec((bk, bn), lambda i, j, k: (k, j))]
    args = [a, b]
    if res is not None:
        in_specs.append(_spec2d(res, (bm, bn), lambda i, j, k: (i, j)))
        args.append(res)
    return pl.pallas_call(
        functools.partial(_mm_kernel, nk=nk, has_res=res is not None),
        out_shape=jax.ShapeDtypeStruct((m, n), out_dtype),
        grid=(m // bm, n // bn, nk),
        in_specs=in_specs,
        out_specs=pl.BlockSpec((bm, bn), lambda i, j, k: (i, j)),
        compiler_params=_params(("parallel", "parallel", "arbitrary")),
        name=name,
    )(*args)


def _mm_wcast_kernel(a_ref, w_ref, o_ref, wb_ref):
    @pl.when(pl.program_id(1) == 0)
    def _():
        wb_ref[...] = w_ref[...].astype(BF16)

    o_ref[...] = jnp.dot(a_ref[...], wb_ref[...], preferred_element_type=F32).astype(o_ref.dtype)


def matmul_wcast(a, w, col0, ncols, *, bm, bn, out_dtype=F32, name="mm_wcast"):
    m, kdim = a.shape
    bm, bn = min(bm, m), min(bn, ncols)
    assert m % bm == 0 and ncols % bn == 0 and col0 % bn == 0 and w.shape[-2] == kdim
    w_spec = _spec2d(w, (kdim, bn), lambda j, i: (0, col0 // bn + j), pipeline_mode=pl.Buffered(1))
    return pl.pallas_call(
        _mm_wcast_kernel,
        out_shape=jax.ShapeDtypeStruct((m, ncols), out_dtype),
        grid=(ncols // bn, m // bm),
        in_specs=[pl.BlockSpec((bm, kdim), lambda j, i: (i, 0)), w_spec],
        out_specs=pl.BlockSpec((bm, bn), lambda j, i: (i, j)),
        scratch_shapes=[pltpu.VMEM((kdim, bn), BF16)],
        compiler_params=_params(("parallel", "arbitrary")),
        name=name,
    )(a, w)


_GELU_K0 = -2.0 * math.sqrt(2.0 / math.pi) * math.log2(math.e)
_GELU_K1 = _GELU_K0 * 0.044715


def _gelu_tanh(x):
    return x / (1.0 + jnp.exp2(x * (_GELU_K0 + _GELU_K1 * (x * x))))


def _rms(x, g):
    return x * lax.rsqrt(jnp.mean(x * x, axis=-1, keepdims=True) + EPS) * g


def _rmsnorm_kernel(x_ref, g_ref, o_ref, *, transpose_out):
    y = _rms(x_ref[...], g_ref[...])
    if transpose_out:
        y = y.T
    o_ref[...] = y.astype(o_ref.dtype)


def rmsnorm(x, g, *, transpose_out, bt=256):
    t, d = x.shape[-2:]
    if transpose_out:
        out_shape, out_spec = (d, t), pl.BlockSpec((d, bt), lambda i: (0, i))
    else:
        out_shape, out_spec = (t, d), pl.BlockSpec((bt, d), lambda i: (i, 0))
    return pl.pallas_call(
        functools.partial(_rmsnorm_kernel, transpose_out=transpose_out),
        out_shape=jax.ShapeDtypeStruct(out_shape, BF16),
        grid=(t // bt,),
        in_specs=[_spec2d(x, (bt, d), lambda i: (i, 0)),
                  pl.BlockSpec((1, d), lambda i: (0, 0))],
        out_specs=out_spec,
        compiler_params=_params(("parallel",)),
        name="rmsnorm_t" if transpose_out else "rmsnorm",
    )(x, g.reshape(1, d))


def _resid_norm_kernel(h_ref, dt_ref, g_ref, *o_refs, want_h):
    h = h_ref[...] + dt_ref[...].T
    if want_h:
        hn_ref, n_ref = o_refs
        hn_ref[...] = h
    else:
        n_ref, = o_refs
    n_ref[...] = _rms(h, g_ref[...]).astype(n_ref.dtype)


def resid_norm(h, delta_t, g, *, want_h, norm_dtype, bt=256):
    t, d = h.shape
    row = pl.BlockSpec((bt, d), lambda i: (i, 0))
    norm_shape = jax.ShapeDtypeStruct((t, d), norm_dtype)
    if want_h:
        out_shape, out_specs = (jax.ShapeDtypeStruct((t, d), F32), norm_shape), (row, row)
    else:
        out_shape = jax.ShapeDtypeStruct((1, t, d), norm_dtype)
        out_specs = pl.BlockSpec((None, bt, d), lambda i: (0, i, 0))
    return pl.pallas_call(
        functools.partial(_resid_norm_kernel, want_h=want_h),
        out_shape=out_shape,
        grid=(t // bt,),
        in_specs=[row, pl.BlockSpec((d, bt), lambda i: (0, i)),
                  pl.BlockSpec((1, d), lambda i: (0, 0))],
        out_specs=out_specs,
        compiler_params=_params(("parallel",)),
        name="resid_norm",
    )(h, delta_t, g.reshape(1, d))


def _shift_rows(p, prev, k):
    rolled = pltpu.roll(p, k, axis=0)
    head = pltpu.roll(prev, k, axis=0)
    rows = lax.broadcasted_iota(jnp.int32, (8, p.shape[1]), 0)
    fixed = jnp.where(rows < k, head, rolled[:8])
    return jnp.concatenate([fixed, rolled[8:]], axis=0)


def _evmix_kernel(proj_ref, cw_ref, ng_ref, sw_ref, sbt_ref, o_ref, carry_ref):
    @pl.when(pl.program_id(0) == 0)
    def _():
        carry_ref[...] = jnp.zeros_like(carry_ref)

    c = CONV_WIDTH
    gb = proj_ref[:, 0:c]
    p = proj_ref[:, c:2 * c] * proj_ref[:, 2 * c:3 * c]
    prev = carry_ref[...]
    conv = p * cw_ref[CONV_K - 1:CONV_K, :]
    for k in range(1, CONV_K):
        conv = conv + _shift_rows(p, prev, k) * cw_ref[CONV_K - 1 - k:CONV_K - k, :]
    carry_ref[...] = p[p.shape[0] - 8:]
    o_ref[:, 0:c] = (gb * conv).astype(o_ref.dtype)

    u = _gelu_tanh(proj_ref[:, 3 * c:3 * c + SGU_WIDTH])
    v = _gelu_tanh(proj_ref[:, 3 * c + SGU_WIDTH:3 * c + 2 * SGU_WIDTH])
    mu = jnp.mean(v, axis=-1, keepdims=True)
    vc = v - mu
    var = jnp.mean(vc * vc, axis=-1, keepdims=True)
    vn = (vc * lax.rsqrt(var + EPS) * ng_ref[...]).astype(BF16)
    q_idx = lax.broadcasted_iota(jnp.int32, (SGU_BLOCK, SGU_BLOCK), 0)
    p_idx = lax.broadcasted_iota(jnp.int32, (SGU_BLOCK, SGU_BLOCK), 1)
    hw = SGU_WIDTH // SGU_HEADS
    for g in range(SGU_HEADS):
        ws = jnp.where(q_idx >= p_idx, sw_ref[g], 0.0).astype(BF16)
        sv = jnp.dot(ws, vn[:, g * hw:(g + 1) * hw], preferred_element_type=F32)
        sv = sv + sbt_ref[:, g:g + 1]
        o_ref[:, c + g * hw:c + (g + 1) * hw] = (u[:, g * hw:(g + 1) * hw] * sv).astype(o_ref.dtype)


def evmix(proj, conv_w, sgu_norm, sgu_w, sgu_b):
    t, width = proj.shape
    bt = SGU_BLOCK
    return pl.pallas_call(
        _evmix_kernel,
        out_shape=jax.ShapeDtypeStruct((t, CONV_WIDTH + SGU_WIDTH), BF16),
        grid=(t // bt,),
        in_specs=[pl.BlockSpec((bt, width), lambda i: (i, 0)),
                  pl.BlockSpec((CONV_K, CONV_WIDTH), lambda i: (0, 0)),
                  pl.BlockSpec((1, SGU_WIDTH), lambda i: (0, 0)),
                  pl.BlockSpec((SGU_HEADS, SGU_BLOCK, SGU_BLOCK), lambda i: (0, 0, 0)),
                  pl.BlockSpec((SGU_BLOCK, SGU_HEADS), lambda i: (0, 0))],
        out_specs=pl.BlockSpec((bt, CONV_WIDTH + SGU_WIDTH), lambda i: (i, 0)),
        scratch_shapes=[pltpu.VMEM((8, CONV_WIDTH), F32)],
        compiler_params=_params(("arbitrary",)),
        name="evmix",
    )(proj, conv_w, sgu_norm.reshape(1, SGU_WIDTH), sgu_w, jnp.transpose(sgu_b))


def _top16(s):
    n, width = s.shape
    rows = lax.broadcasted_iota(jnp.int32, s.shape, 0).astype(F32)
    slot = lax.broadcasted_iota(jnp.int32, (PEER_TOPK, width), 0)
    cur = s
    rank = jnp.full(s.shape, float(PEER_TOPK), F32)
    vals = jnp.zeros((PEER_TOPK, width), F32)
    for r in range(PEER_TOPK):
        m = jnp.max(cur, axis=0, keepdims=True)
        first = jnp.min(jnp.where(cur == m, rows, float(n)), axis=0, keepdims=True)
        sel = rows == first
        rank = jnp.where(sel, float(r), rank)
        cur = jnp.where(sel, -jnp.inf, cur)
        vals = jnp.where(slot == r, m, vals)
    return rank, vals


def _top16_distinct(s):
    ax = s.ndim - 2
    slot = lax.broadcasted_iota(jnp.int32, s.shape[:ax] + (PEER_TOPK, s.shape[-1]), ax)
    cur = s
    vals = jnp.zeros(slot.shape, F32)
    for r in range(PEER_TOPK):
        m = jnp.max(cur, axis=ax, keepdims=True)
        cur = jnp.where(cur == m, -jnp.inf, cur)
        vals = jnp.where(slot == r, m, vals)
    last = vals[..., PEER_TOPK - 1:PEER_TOPK, :]
    count = jnp.sum(jnp.where(s >= last, 1.0, 0.0), axis=ax, keepdims=True)
    return vals, count


def _route_kernel(qt_ref, keys_ref, seta_ref, setb_ref, rowsel_ref,
                  lrow_ref, e1_ref, rank2_ref, e2_ref,
                  s_ref, t_ref, n_ref, cand_ref, tc_ref, nc_ref):
    H = PEER_HEADS
    k = float(PEER_TOPK)

    def candidates(v1, v2):
        cand = (jnp.dot(seta_ref[...], v1, preferred_element_type=F32, precision=lax.Precision.HIGHEST)
                + jnp.dot(setb_ref[...], v2, preferred_element_type=F32, precision=lax.Precision.HIGHEST))
        crow = lax.broadcasted_iota(jnp.int32, cand.shape, 0)
        return jnp.where(crow < PEER_NCAND, cand, -jnp.inf)

    for h in range(H):
        for side in range(2):
            q = qt_ref[(2 * h + side) * PEER_HALF:(2 * h + side + 1) * PEER_HALF, :].astype(BF16)
            s_ref[side * H + h] = jnp.dot(keys_ref[h, side], q, preferred_element_type=F32)
    t_all, n_all = _top16_distinct(s_ref[...])
    t_ref[...] = t_all
    n_ref[...] = n_all
    for h in range(H):
        cand_ref[h] = candidates(t_ref[h], t_ref[H + h])
    tc_all, nc_all = _top16_distinct(cand_ref[...])
    tc_ref[...] = tc_all
    nc_ref[...] = nc_all

    def head(h, carry):
        s1 = s_ref[h]
        s2 = s_ref[H + h]

        def finish(cand, chosen, best, lrow_of, rank2):
            z = jnp.sum(jnp.where(chosen, jnp.exp(cand - best), 0.0), axis=0, keepdims=True)
            length = jnp.dot(rowsel_ref[...], chosen.astype(F32), preferred_element_type=F32)
            lrow = jnp.zeros_like(s1)
            for a in range(PEER_TOPK):
                lrow = jnp.where(lrow_of(a), length[a:a + 1, :], lrow)
            lrow_ref[h] = lrow
            rank2_ref[h] = rank2
            e2_ref[h] = jnp.exp(s2 - jnp.max(s2, axis=0, keepdims=True)) / z

        e1_ref[h] = jnp.exp(s1 - jnp.max(s1, axis=0, keepdims=True))
        ties = jnp.max(jnp.abs(n_ref[h] - k) + jnp.abs(n_ref[H + h] - k) + jnp.abs(nc_ref[h] - k))

        @pl.when(ties == 0.0)
        def _():
            t1 = t_ref[h]
            t2 = t_ref[H + h]
            tc = tc_ref[h]
            cand = cand_ref[h]
            rank2 = jnp.full(s2.shape, k, F32)
            for r in range(PEER_TOPK):
                rank2 = jnp.where(s2 == t2[r:r + 1, :], float(r), rank2)
            finish(cand, cand >= tc[PEER_TOPK - 1:PEER_TOPK, :], tc[0:1, :],
                   lambda a: s1 == t1[a:a + 1, :], rank2)

        @pl.when(ties != 0.0)
        def _():
            rank1, v1 = _top16(s1)
            rank2, v2 = _top16(s2)
            cand_exact = candidates(v1, v2)
            crank, cvals = _top16(cand_exact)
            finish(cand_exact, crank < k, cvals[0:1, :], lambda a: rank1 == float(a), rank2)

        return carry

    any_ties = jnp.max(jnp.abs(n_all[:H] - k) + jnp.abs(n_all[H:] - k) + jnp.abs(nc_all - k))

    @pl.when(any_ties == 0.0)
    def _():
        s1 = s_ref[0:H]
        s2 = s_ref[H:2 * H]
        t1 = t_ref[0:H]
        t2 = t_ref[H:2 * H]
        tc = tc_ref[...]
        cand = cand_ref[...]
        chosen = cand >= tc[:, PEER_TOPK - 1:PEER_TOPK, :]
        z = jnp.sum(jnp.where(chosen, jnp.exp(cand - tc[:, 0:1, :]), 0.0), axis=1, keepdims=True)
        chosen_f = chosen.astype(F32)
        length = jnp.stack([jnp.dot(rowsel_ref[...], chosen_f[h], preferred_element_type=F32)
                            for h in range(H)], axis=0)
        lrow = jnp.zeros_like(s1)
        rank2 = jnp.full(s2.shape, k, F32)
        for a in range(PEER_TOPK):
            lrow = jnp.where(s1 == t1[:, a:a + 1, :], length[:, a:a + 1, :], lrow)
            rank2 = jnp.where(s2 == t2[:, a:a + 1, :], float(a), rank2)
        lrow_ref[...] = lrow
        rank2_ref[...] = rank2
        e1_ref[...] = jnp.exp(s1 - t1[:, 0:1, :])
        e2_ref[...] = jnp.exp(s2 - t2[:, 0:1, :]) / z

    @pl.when(any_ties != 0.0)
    def _():
        lax.fori_loop(0, PEER_HEADS, head, 0)


def peer_route(qt, keys_bf16):
    _, t = qt.shape
    bt = LANES
    seta = np.zeros((PEER_NCAND_PAD, PEER_TOPK), np.float32)
    setb = np.zeros((PEER_NCAND_PAD, PEER_TOPK), np.float32)
    rowsel = np.zeros((PEER_TOPK, PEER_NCAND_PAD), np.float32)
    for k, (a, b) in enumerate(PEER_CAND):
        seta[k, a] = 1.0
        setb[k, b] = 1.0
        rowsel[a, k] = 1.0
    table = jax.ShapeDtypeStruct((PEER_HEADS, PEER_NKEYS, t), F32)
    table16 = jax.ShapeDtypeStruct((PEER_HEADS, PEER_NKEYS, t), BF16)
    tspec = pl.BlockSpec((PEER_HEADS, PEER_NKEYS, bt), lambda i: (0, 0, i))
    const2 = lambda i: (0, 0)
    return pl.pallas_call(
        _route_kernel,
        out_shape=(table, table, table, table),
        grid=(t // bt,),
        in_specs=[pl.BlockSpec((PEER_HEADS * 2 * PEER_HALF, bt), lambda i: (0, i)),
                  pl.BlockSpec((PEER_HEADS, 2, PEER_NKEYS, PEER_HALF), lambda i: (0, 0, 0, 0)),
                  pl.BlockSpec((PEER_NCAND_PAD, PEER_TOPK), const2),
                  pl.BlockSpec((PEER_NCAND_PAD, PEER_TOPK), const2),
                  pl.BlockSpec((PEER_TOPK, PEER_NCAND_PAD), const2)],
        out_specs=(tspec, tspec, tspec, tspec),
        scratch_shapes=[pltpu.VMEM((2 * PEER_HEADS, PEER_NKEYS, bt), F32),
                        pltpu.VMEM((2 * PEER_HEADS, PEER_TOPK, bt), F32),
                        pltpu.VMEM((2 * PEER_HEADS, 1, bt), F32),
                        pltpu.VMEM((PEER_HEADS, PEER_NCAND_PAD, bt), F32),
                        pltpu.VMEM((PEER_HEADS, PEER_TOPK, bt), F32),
                        pltpu.VMEM((PEER_HEADS, 1, bt), F32)],
        compiler_params=_params(("parallel",)),
        name="peer_route",
    )(qt, keys_bf16, seta, setb, rowsel)


def _peer_act_kernel(u_ref, xt_ref, o_ref):
    hid = jnp.dot(u_ref[...], xt_ref[...], preferred_element_type=F32)
    o_ref[...] = _gelu_tanh(hid).astype(o_ref.dtype)


def peer_act(u_bf16, xt, *, bt=512, be=2048):
    e, d = u_bf16.shape
    _, t = xt.shape
    return pl.pallas_call(
        _peer_act_kernel,
        out_shape=jax.ShapeDtypeStruct((e, t), BF16),
        grid=(t // bt, e // be),
        in_specs=[pl.BlockSpec((be, d), lambda i, j: (j, 0)),
                  pl.BlockSpec((d, bt), lambda i, j: (0, i))],
        out_specs=pl.BlockSpec((be, bt), lambda i, j: (j, i)),
        compiler_params=_params(("parallel", "arbitrary")),
        name="peer_act",
    )(u_bf16, xt)


def _peer_out_kernel(vt_ref, a_ref, lrow_ref, e1_ref, rank2_ref, e2_ref, o_ref, p_ref, *, nsub):
    e = pl.program_id(1)
    bt = o_ref.shape[1]
    nparts = PEER_NKEYS // BF16_ROWS

    @pl.when(e == 0)
    def _():
        o_ref[...] = jnp.zeros_like(o_ref)

    first_rows = [[(lrow_ref[h, pl.ds(e * nsub + ii, 1), :], e1_ref[h, pl.ds(e * nsub + ii, 1), :])
                   for ii in range(nsub)] for h in range(PEER_HEADS)]
    for tb, first in ((tb, first) for tb in range(bt // LANES) for first in range(0, nsub, PEER_SLAB_GROUP)):
        cols = slice(tb * LANES, (tb + 1) * LANES)
        slabs = range(first, first + PEER_SLAB_GROUP)
        acc = {ii: [None] * nparts for ii in slabs}
        for h in range(PEER_HEADS):
            bcast = lambda row: jnp.broadcast_to(row[:, cols], (BF16_ROWS, LANES)).astype(BF16)
            length = {ii: bcast(first_rows[h][ii][0]) for ii in slabs}
            gate1 = {ii: bcast(first_rows[h][ii][1]) for ii in slabs}
            for p in range(nparts):
                rows = slice(p * BF16_ROWS, (p + 1) * BF16_ROWS)
                rank2 = rank2_ref[h, rows, cols].astype(BF16)
                gate2 = e2_ref[h, rows, cols].astype(BF16)
                for ii in slabs:
                    contrib = jnp.where(rank2 < length[ii], gate2, jnp.zeros_like(gate2)) * gate1[ii]
                    acc[ii][p] = contrib if acc[ii][p] is None else acc[ii][p] + contrib
        for ii in slabs:
            for p in range(nparts):
                rows = slice(ii * PEER_NKEYS + p * BF16_ROWS, ii * PEER_NKEYS + (p + 1) * BF16_ROWS)
                p_ref[rows, cols] = acc[ii][p] * a_ref[rows, cols]
    o_ref[...] += jnp.dot(vt_ref[...], p_ref[...], preferred_element_type=F32)


def peer_out(v_t, at, tables, *, bt=512, be=512):
    d, e = v_t.shape
    _, t = at.shape
    tspec = pl.BlockSpec((PEER_HEADS, PEER_NKEYS, bt), lambda i, j: (0, 0, i))
    return pl.pallas_call(
        functools.partial(_peer_out_kernel, nsub=be // PEER_NKEYS),
        out_shape=jax.ShapeDtypeStruct((d, t), F32),
        grid=(t // bt, e // be),
        in_specs=[pl.BlockSpec((d, be), lambda i, j: (0, j)),
                  pl.BlockSpec((be, bt), lambda i, j: (j, i)),
                  tspec, tspec, tspec, tspec],
        out_specs=pl.BlockSpec((d, bt), lambda i, j: (0, i)),
        scratch_shapes=[pltpu.VMEM((be, bt), BF16)],
        compiler_params=_params(("parallel", "arbitrary")),
        name="peer_out",
    )(v_t, at, *tables)


def peer_ffn_t(xt, wq_t, keys_bf16, u_bf16, v_t):
    qt = matmul(wq_t, xt, bm=1024, bn=1024, name="peer_q")
    tables = peer_route(qt, keys_bf16)
    return peer_out(v_t, peer_act(u_bf16, xt), tables)


def _conv_silu_kernel(x_ref, w_ref, b_ref, o_ref, carry_ref):
    @pl.when(pl.program_id(1) == 0)
    def _():
        carry_ref[...] = jnp.zeros_like(carry_ref)

    x = x_ref[...]
    prev = carry_ref[...]
    acc = x * w_ref[SSM_CONV_K - 1:SSM_CONV_K, :] + b_ref[...]
    for k in range(1, SSM_CONV_K):
        acc = acc + _shift_rows(x, prev, k) * w_ref[SSM_CONV_K - 1 - k:SSM_CONV_K - k, :]
    carry_ref[...] = x[x.shape[0] - 8:]
    o_ref[...] = jax.nn.silu(acc).astype(o_ref.dtype)


def conv_silu(x, w, b, *, bt=1024, bc=1024):
    t, c = x.shape
    bt = min(bt, t)
    return pl.pallas_call(
        _conv_silu_kernel,
        out_shape=jax.ShapeDtypeStruct((t, c), BF16),
        grid=(c // bc, t // bt),
        in_specs=[pl.BlockSpec((bt, bc), lambda j, i: (i, j)),
                  pl.BlockSpec((SSM_CONV_K, bc), lambda j, i: (0, j)),
                  pl.BlockSpec((1, bc), lambda j, i: (0, j))],
        out_specs=pl.BlockSpec((bt, bc), lambda j, i: (i, j)),
        scratch_shapes=[pltpu.VMEM((8, bc), F32)],
        compiler_params=_params(("parallel", "arbitrary")),
        name="conv_silu",
    )(x, w, b.reshape(1, c))


def _dt_prep_kernel(dt_ref, bias_ref, alog_ref, acs_ref, acst_ref, rowp_ref):
    L = SSD_CHUNK
    dt = jax.nn.softplus(dt_ref[...] + bias_ref[...])
    da = dt * (-jnp.exp(alog_ref[...]))
    li = lax.broadcasted_iota(jnp.int32, (L, L), 0)
    si = lax.broadcasted_iota(jnp.int32, (L, L), 1)
    acs = jnp.dot((li >= si).astype(F32), da, preferred_element_type=F32,
                  precision=lax.Precision.HIGHEST) * math.log2(math.e)
    acs_ref[...] = acs
    acst_ref[...] = acs.T
    rowp_ref[...] = (acs - jnp.log2(dt)).T


def dt_prep(dt_raw, dt_bias, a_log):
    t, heads = dt_raw.shape
    L = SSD_CHUNK
    vec = pl.BlockSpec((1, heads), lambda c: (0, 0))
    tr = pl.BlockSpec((heads, L), lambda c: (0, c))
    return pl.pallas_call(
        _dt_prep_kernel,
        out_shape=(jax.ShapeDtypeStruct((t, heads), F32), jax.ShapeDtypeStruct((heads, t), F32),
                   jax.ShapeDtypeStruct((heads, t), F32)),
        grid=(t // L,),
        in_specs=[pl.BlockSpec((L, heads), lambda c: (c, 0)), vec, vec],
        out_specs=(pl.BlockSpec((L, heads), lambda c: (c, 0)), tr, tr),
        compiler_params=_params(("parallel",)),
        name="dt_prep",
    )(dt_raw, dt_bias.reshape(1, heads), a_log.reshape(1, heads))


def _ssd_kernel(x_ref, b_ref, c_ref, acs_ref, acst_ref, rowp_ref,
                dskip_ref, z_ref, g_ref, y_ref, state_ref, rhs_ref, gated_ref):
    L = SSD_CHUNK
    P = SSM_HEADDIM
    R = SSM_HEADS // SSM_GROUPS
    sumsq = None

    @pl.when(pl.program_id(1) == 0)
    def _():
        state_ref[...] = jnp.zeros_like(state_ref)
        rhs_ref[...] = jnp.zeros_like(rhs_ref)

    li = lax.broadcasted_iota(jnp.int32, (L, L), 0)
    si = lax.broadcasted_iota(jnp.int32, (L, L), 1)
    causal = li >= si
    acs2 = acs_ref[0]
    acst2 = acst_ref[0]
    rowp_all = rowp_ref[0]

    bmat = b_ref[...].astype(F32)
    cmat = c_ref[...].astype(F32)
    cb = lax.dot_general(cmat.astype(BF16), bmat.astype(BF16), (((1,), (1,)), ((), ())),
                         preferred_element_type=F32)
    bt = bmat.T

    lane = lax.broadcasted_iota(jnp.int32, (L, LANES), 1)
    low = lane < P
    zeros_b = jnp.zeros((L, LANES), BF16)
    for tile in range(R // 4):
        for pair in range(2):
            cols = slice(tile * 256 + pair * LANES, tile * 256 + (pair + 1) * LANES)
            xb = x_ref[:, cols].astype(BF16)
            sb = state_ref[:, cols].astype(BF16)
            dst = slice(pair * LANES, (pair + 1) * LANES)
            for k, keep in enumerate((low, ~low)):
                base = (2 * pair + k) * L
                rhs_ref[tile, base:base + L, dst] = jnp.where(keep, xb, zeros_b)
                rhs_ref[tile, 4 * L + base:4 * L + base + L, dst] = jnp.where(keep, sb, zeros_b)

    head_of_lane = lax.broadcasted_iota(jnp.int32, (1, R * P), 1) // P
    state_decay = jnp.zeros((1, R * P), F32)
    lhs_y, lhs_state = [], []
    for tile in range(R // 4):
        on_x, on_state, to_state = [], [], []
        for r in range(4 * tile, 4 * tile + 4):
            col = acs2[:, r:r + 1]
            rowp = rowp_all[r:r + 1, :]
            last = acst2[r:r + 1, L - 1:L]
            on_x.append((cb * jnp.exp2(jnp.where(causal, col - rowp, -jnp.inf))).astype(BF16))
            on_state.append((cmat * jnp.exp2(col)).astype(BF16))
            to_state.append((bt * jnp.exp2(last - rowp)).astype(BF16))
            state_decay = jnp.where(head_of_lane == r, jnp.exp2(last), state_decay)
        lhs_y.append(jnp.concatenate(on_x + on_state, axis=1))
        lhs_state.append(jnp.concatenate(to_state, axis=1))
    y_mix = [jnp.dot(lhs_y[tile], rhs_ref[tile], preferred_element_type=F32)
             for tile in range(R // 4)]
    new_state = [jnp.dot(lhs_state[tile], rhs_ref[tile, 0:4 * L, :], preferred_element_type=F32)
                 for tile in range(R // 4)]
    for tile in range(R // 4):
        cols = slice(tile * 256, (tile + 1) * 256)
        y = y_mix[tile] + dskip_ref[0][:, cols] * x_ref[:, cols].astype(F32)
        gated = y * jax.nn.silu(z_ref[:, cols])
        gated_ref[:, cols] = gated
        part = jnp.sum(gated * gated, axis=-1, keepdims=True)
        sumsq = part if sumsq is None else sumsq + part
    for tile in range(R // 4):
        cols = slice(tile * 256, (tile + 1) * 256)
        state_ref[:, cols] = state_ref[:, cols] * state_decay[:, cols] + new_state[tile]
    scale = lax.rsqrt(sumsq * (1.0 / (R * P)) + EPS)
    y_ref[...] = (gated_ref[...] * scale * g_ref[...]).astype(y_ref.dtype)


def ssd(xbc, dt_raw, dt_bias, a_log, d_skip, z, norm_g):
    t = xbc.shape[0]
    G, R, L = SSM_GROUPS, SSM_HEADS // SSM_GROUPS, SSD_CHUNK
    gw = R * SSM_HEADDIM
    acs, acst, rowp = dt_prep(dt_raw, dt_bias, a_log)
    acs = acs.reshape(t, G, R).transpose(1, 0, 2)
    acst = acst.reshape(G, R, t)
    rowp = rowp.reshape(G, R, t)
    dskip = jnp.repeat(d_skip, SSM_HEADDIM).reshape(G, 1, gw)
    per_time = pl.BlockSpec((1, R, L), lambda g, c: (g, 0, c))
    nb = SSM_INNER // SSM_STATE
    return pl.pallas_call(
        _ssd_kernel,
        out_shape=jax.ShapeDtypeStruct((t, SSM_INNER), BF16),
        grid=(G, t // L),
        in_specs=[pl.BlockSpec((L, gw), lambda g, c: (c, g)),
                  pl.BlockSpec((L, SSM_STATE), lambda g, c: (c, nb + g)),
                  pl.BlockSpec((L, SSM_STATE), lambda g, c: (c, nb + G + g)),
                  pl.BlockSpec((1, L, R), lambda g, c: (g, c, 0)), per_time, per_time,
                  pl.BlockSpec((1, 1, gw), lambda g, c: (g, 0, 0)),
                  pl.BlockSpec((L, gw), lambda g, c: (c, g)),
                  pl.BlockSpec((1, gw), lambda g, c: (0, g))],
        out_specs=pl.BlockSpec((L, gw), lambda g, c: (c, g)),
        scratch_shapes=[pltpu.VMEM((SSM_STATE, gw), F32),
                        pltpu.VMEM((R // 4, 4 * 2 * L, 256), BF16),
                        pltpu.VMEM((L, gw), F32)],
        compiler_params=_params(("parallel", "arbitrary")),
        name="ssd",
    )(xbc, xbc, xbc, acs, acst, rowp, dskip, z, norm_g.reshape(1, SSM_INNER))


def kernel(x, mix_norm, ffn_norm, final_norm, ev_w_in, ev_conv_w, ev_sgu_norm, ev_sgu_w, ev_sgu_b, ev_w_out, od_w_in, od_conv_w, od_conv_b, od_dt_bias, od_a_log, od_d, od_norm, od_w_out, peer_wq, peer_keys, peer_u, peer_v):
    h = x

    def peer(xt, i):
        return peer_ffn_t(xt, peer_wq[i].T.astype(BF16), peer_keys[i].astype(BF16),
                          peer_u[i].astype(BF16), peer_v[i].T.astype(BF16))

    hn = rmsnorm(h, mix_norm[0], transpose_out=False)
    proj = matmul_wcast(hn, ev_w_in, 0, ev_w_in.shape[2], bm=1024, bn=1024, name="ev_in")
    ycat = evmix(proj, ev_conv_w[0], ev_sgu_norm[0], ev_sgu_w[0], ev_sgu_b[0])
    h = matmul(ycat, ev_w_out[0].astype(BF16), bm=1024, bn=1024, res=h, name="ev_out")
    delta_t = peer(rmsnorm(h, ffn_norm[0], transpose_out=True), 0)

    h, hn = resid_norm(h, delta_t, mix_norm[1], want_h=True, norm_dtype=BF16)
    w_in = od_w_in
    z = matmul_wcast(hn, w_in, 0, SSM_INNER, bm=1024, bn=1024, name="od_in_z")
    xbc = matmul_wcast(hn, w_in, SSM_INNER, SSM_CONV_DIM, bm=1024, bn=1024, name="od_in_xbc")
    dt_raw = matmul_wcast(hn, w_in, SSM_INNER + SSM_CONV_DIM, SSM_HEADS, bm=1024, bn=SSM_HEADS,
                          name="od_in_dt")
    xbc = conv_silu(xbc, od_conv_w[0], od_conv_b[0])
    yn = ssd(xbc, dt_raw, od_dt_bias[0], od_a_log[0], od_d[0], z, od_norm[0])
    h = matmul(yn, od_w_out[0].astype(BF16), bm=1024, bn=1024, bk=2048, res=h, name="od_out")
    delta_t = peer(rmsnorm(h, ffn_norm[1], transpose_out=True), 1)

    out = resid_norm(h, delta_t, final_norm, want_h=False, norm_dtype=F32)
    return out
```

```python
import functools
import math

import jax
import jax.numpy as jnp
import numpy as np
from jax import lax
from jax.experimental import pallas as pl
from jax.experimental.pallas import tpu as pltpu

F32 = jnp.float32
BF16 = jnp.bfloat16

EPS = 1e-6
LANES = 128
BF16_ROWS = 16
VMEM_LIMIT = 56 * 1024 * 1024

D_MODEL = 4096
CONV_WIDTH = 2048
CONV_K = 3
SGU_WIDTH = 2048
SGU_HEADS = 16
SGU_BLOCK = 128
SSM_INNER = 8192
SSM_HEADDIM = 64
SSM_HEADS = 128
SSM_GROUPS = 8
SSM_STATE = 128
SSM_CONV_K = 4
SSM_CONV_DIM = SSM_INNER + 2 * SSM_GROUPS * SSM_STATE
SSD_CHUNK = 128
PEER_HEADS = 8
PEER_NKEYS = 128
PEER_HALF = 128
PEER_TOPK = 16
PEER_CAND = [(a, b) for a in range(PEER_TOPK) for b in range(PEER_TOPK)
             if (a + 1) * (b + 1) <= PEER_TOPK]
PEER_NCAND = len(PEER_CAND)
PEER_NCAND_PAD = -(-PEER_NCAND // 8) * 8
PEER_SLAB_GROUP = 4


def _spec2d(arr, block, index_map, lead=0, **kwargs):
    if arr.ndim == 2:
        return pl.BlockSpec(block, index_map, **kwargs)
    assert arr.ndim == 3 and lead < arr.shape[0]
    return pl.BlockSpec((None,) + tuple(block), lambda *idx: (lead,) + tuple(index_map(*idx)), **kwargs)


def _params(sem):
    return pltpu.CompilerParams(dimension_semantics=sem, vmem_limit_bytes=VMEM_LIMIT)


def _mm_kernel(*refs, nk, has_res):
    if has_res:
        a_ref, b_ref, r_ref, o_ref = refs
    else:
        a_ref, b_ref, o_ref = refs
        r_ref = None

    if nk == 1:
        part = jnp.dot(a_ref[...], b_ref[...], preferred_element_type=F32)
        if has_res:
            part = part + r_ref[...]
        o_ref[...] = part.astype(o_ref.dtype)
    else:
        @pl.when(pl.program_id(2) == 0)
        def _():
            o_ref[...] = r_ref[...] if has_res else jnp.zeros_like(o_ref)

        o_ref[...] += jnp.dot(a_ref[...], b_ref[...], preferred_element_type=F32)


def matmul(a, b, *, bm, bn, bk=None, out_dtype=F32, res=None, name="mm"):
    m, kdim = a.shape
    _, n = b.shape
    bk = kdim if bk is None else bk
    bm, bn = min(bm, m), min(bn, n)
    nk = kdim // bk
    assert m % bm == 0 and n % bn == 0 and kdim % bk == 0
    assert nk == 1 or out_dtype == F32
    in_specs = [pl.BlockSpec((bm, bk), lambda i, j, k: (i, k)),
                pl.BlockSpec((bk, bn), lambda i, j, k: (k, j))]
    args = [a, b]
    if res is not None:
        in_specs.append(_spec2d(res, (bm, bn), lambda i, j, k: (i, j)))
        args.append(res)
    return pl.pallas_call(
        functools.partial(_mm_kernel, nk=nk, has_res=res is not None),
        out_shape=jax.ShapeDtypeStruct((m, n), out_dtype),
        grid=(m // bm, n // bn, nk),
        in_specs=in_specs,
        out_specs=pl.BlockSpec((bm, bn), lambda i, j, k: (i, j)),
        compiler_params=_params(("parallel", "parallel", "arbitrary")),
        name=name,
    )(*args)


def _mm_wcast_kernel(a_ref, w_ref, o_ref, wb_ref):
    @pl.when(pl.program_id(1) == 0)
    def _():
        wb_ref[...] = w_ref[...].astype(BF16)

    o_ref[...] = jnp.dot(a_ref[...], wb_ref[...], preferred_element_type=F32).astype(o_ref.dtype)


def matmul_wcast(a, w, col0, ncols, *, bm, bn, out_dtype=F32, name="mm_wcast"):
    m, kdim = a.shape
    bm, bn = min(bm, m), min(bn, ncols)
    assert m % bm == 0 and ncols % bn == 0 and col0 % bn == 0 and w.shape[-2] == kdim
    w_spec = _spec2d(w, (kdim, bn), lambda j, i: (0, col0 // bn + j), pipeline_mode=pl.Buffered(1))
    return pl.pallas_call(
        _mm_wcast_kernel,
        out_shape=jax.ShapeDtypeStruct((m, ncols), out_dtype),
        grid=(ncols // bn, m // bm),
        in_specs=[pl.BlockSpec((bm, kdim), lambda j, i: (i, 0)), w_spec],
        out_specs=pl.BlockSpec((bm, bn), lambda j, i: (i, j)),
        scratch_shapes=[pltpu.VMEM((kdim, bn), BF16)],
        compiler_params=_params(("parallel", "arbitrary")),
        name=name,
    )(a, w)


_GELU_K0 = -2.0 * math.sqrt(2.0 / math.pi) * math.log2(math.e)
_GELU_K1 = _GELU_K0 * 0.044715


def _gelu_tanh(x):
    return x / (1.0 + jnp.exp2(x * (_GELU_K0 + _GELU_K1 * (x * x))))


def _rms(x, g):
    return x * lax.rsqrt(jnp.mean(x * x, axis=-1, keepdims=True) + EPS) * g


def _rmsnorm_kernel(x_ref, g_ref, o_ref, *, transpose_out):
    y = _rms(x_ref[...], g_ref[...])
    if transpose_out:
        y = y.T
    o_ref[...] = y.astype(o_ref.dtype)


def rmsnorm(x, g, *, transpose_out, bt=256):
    t, d = x.shape[-2:]
    if transpose_out:
        out_shape, out_spec = (d, t), pl.BlockSpec((d, bt), lambda i: (0, i))
    else:
        out_shape, out_spec = (t, d), pl.BlockSpec((bt, d), lambda i: (i, 0))
    return pl.pallas_call(
        functools.partial(_rmsnorm_kernel, transpose_out=transpose_out),
        out_shape=jax.ShapeDtypeStruct(out_shape, BF16),
        grid=(t // bt,),
        in_specs=[_spec2d(x, (bt, d), lambda i: (i, 0)),
                  pl.BlockSpec((1, d), lambda i: (0, 0))],
        out_specs=out_spec,
        compiler_params=_params(("parallel",)),
        name="rmsnorm_t" if transpose_out else "rmsnorm",
    )(x, g.reshape(1, d))


def _resid_norm_kernel(h_ref, dt_ref, g_ref, *o_refs, want_h):
    h = h_ref[...] + dt_ref[...].T
    if want_h:
        hn_ref, n_ref = o_refs
        hn_ref[...] = h
    else:
        n_ref, = o_refs
    n_ref[...] = _rms(h, g_ref[...]).astype(n_ref.dtype)


def resid_norm(h, delta_t, g, *, want_h, norm_dtype, bt=256):
    t, d = h.shape
    row = pl.BlockSpec((bt, d), lambda i: (i, 0))
    norm_shape = jax.ShapeDtypeStruct((t, d), norm_dtype)
    if want_h:
        out_shape, out_specs = (jax.ShapeDtypeStruct((t, d), F32), norm_shape), (row, row)
    else:
        out_shape = jax.ShapeDtypeStruct((1, t, d), norm_dtype)
        out_specs = pl.BlockSpec((None, bt, d), lambda i: (0, i, 0))
    return pl.pallas_call(
        functools.partial(_resid_norm_kernel, want_h=want_h),
        out_shape=out_shape,
        grid=(t // bt,),
        in_specs=[row, pl.BlockSpec((d, bt), lambda i: (0, i)),
                  pl.BlockSpec((1, d), lambda i: (0, 0))],
        out_specs=out_specs,
        compiler_params=_params(("parallel",)),
        name="resid_norm",
    )(h, delta_t, g.reshape(1, d))


def _shift_rows(p, prev, k):
    rolled = pltpu.roll(p, k, axis=0)
    head = pltpu.roll(prev, k, axis=0)
    rows = lax.broadcasted_iota(jnp.int32, (8, p.shape[1]), 0)
    fixed = jnp.where(rows < k, head, rolled[:8])
    return jnp.concatenate([fixed, rolled[8:]], axis=0)


def _evmix_kernel(proj_ref, cw_ref, ng_ref, sw_ref, sbt_ref, o_ref, carry_ref):
    @pl.when(pl.program_id(0) == 0)
    def _():
        carry_ref[...] = jnp.zeros_like(carry_ref)

    c = CONV_WIDTH
    gb = proj_ref[:, 0:c]
    p = proj_ref[:, c:2 * c] * proj_ref[:, 2 * c:3 * c]
    prev = carry_ref[...]
    conv = p * cw_ref[CONV_K - 1:CONV_K, :]
    for k in range(1, CONV_K):
        conv = conv + _shift_rows(p, prev, k) * cw_ref[CONV_K - 1 - k:CONV_K - k, :]
    carry_ref[...] = p[p.shape[0] - 8:]
    o_ref[:, 0:c] = (gb * conv).astype(o_ref.dtype)

    u = _gelu_tanh(proj_ref[:, 3 * c:3 * c + SGU_WIDTH])
    v = _gelu_tanh(proj_ref[:, 3 * c + SGU_WIDTH:3 * c + 2 * SGU_WIDTH])
    mu = jnp.mean(v, axis=-1, keepdims=True)
    vc = v - mu
    var = jnp.mean(vc * vc, axis=-1, keepdims=True)
    vn = (vc * lax.rsqrt(var + EPS) * ng_ref[...]).astype(BF16)
    q_idx = lax.broadcasted_iota(jnp.int32, (SGU_BLOCK, SGU_BLOCK), 0)
    p_idx = lax.broadcasted_iota(jnp.int32, (SGU_BLOCK, SGU_BLOCK), 1)
    hw = SGU_WIDTH // SGU_HEADS
    for g in range(SGU_HEADS):
        ws = jnp.where(q_idx >= p_idx, sw_ref[g], 0.0).astype(BF16)
        sv = jnp.dot(ws, vn[:, g * hw:(g + 1) * hw], preferred_element_type=F32)
        sv = sv + sbt_ref[:, g:g + 1]
        o_ref[:, c + g * hw:c + (g + 1) * hw] = (u[:, g * hw:(g + 1) * hw] * sv).astype(o_ref.dtype)


def evmix(proj, conv_w, sgu_norm, sgu_w, sgu_b):
    t, width = proj.shape
    bt = SGU_BLOCK
    return pl.pallas_call(
        _evmix_kernel,
        out_shape=jax.ShapeDtypeStruct((t, CONV_WIDTH + SGU_WIDTH), BF16),
        grid=(t // bt,),
        in_specs=[pl.BlockSpec((bt, width), lambda i: (i, 0)),
                  pl.BlockSpec((CONV_K, CONV_WIDTH), lambda i: (0, 0)),
                  pl.BlockSpec((1, SGU_WIDTH), lambda i: (0, 0)),
                  pl.BlockSpec((SGU_HEADS, SGU_BLOCK, SGU_BLOCK), lambda i: (0, 0, 0)),
                  pl.BlockSpec((SGU_BLOCK, SGU_HEADS), lambda i: (0, 0))],
        out_specs=pl.BlockSpec((bt, CONV_WIDTH + SGU_WIDTH), lambda i: (i, 0)),
        scratch_shapes=[pltpu.VMEM((8, CONV_WIDTH), F32)],
        compiler_params=_params(("arbitrary",)),
        name="evmix",
    )(proj, conv_w, sgu_norm.reshape(1, SGU_WIDTH), sgu_w, jnp.transpose(sgu_b))


def _top16(s):
    n, width = s.shape
    rows = lax.broadcasted_iota(jnp.int32, s.shape, 0).astype(F32)
    slot = lax.broadcasted_iota(jnp.int32, (PEER_TOPK, width), 0)
    cur = s
    rank = jnp.full(s.shape, float(PEER_TOPK), F32)
    vals = jnp.zeros((PEER_TOPK, width), F32)
    for r in range(PEER_TOPK):
        m = jnp.max(cur, axis=0, keepdims=True)
        first = jnp.min(jnp.where(cur == m, rows, float(n)), axis=0, keepdims=True)
        sel = rows == first
        rank = jnp.where(sel, float(r), rank)
        cur = jnp.where(sel, -jnp.inf, cur)
        vals = jnp.where(slot == r, m, vals)
    return rank, vals


def _top16_distinct(s):
    ax = s.ndim - 2
    slot = lax.broadcasted_iota(jnp.int32, s.shape[:ax] + (PEER_TOPK, s.shape[-1]), ax)
    cur = s
    vals = jnp.zeros(slot.shape, F32)
    for r in range(PEER_TOPK):
        m = jnp.max(cur, axis=ax, keepdims=True)
        cur = jnp.where(cur == m, -jnp.inf, cur)
        vals = jnp.where(slot == r, m, vals)
    last = vals[..., PEER_TOPK - 1:PEER_TOPK, :]
    count = jnp.sum(jnp.where(s >= last, 1.0, 0.0), axis=ax, keepdims=True)
    return vals, count


def _route_kernel(qt_ref, keys_ref, seta_ref, setb_ref, rowsel_ref,
                  lrow_ref, e1_ref, rank2_ref, e2_ref,
                  s_ref, t_ref, n_ref, cand_ref, tc_ref, nc_ref):
    H = PEER_HEADS
    k = float(PEER_TOPK)

    def candidates(v1, v2):
        cand = (jnp.dot(seta_ref[...], v1, preferred_element_type=F32, precision=lax.Precision.HIGHEST)
                + jnp.dot(setb_ref[...], v2, preferred_element_type=F32, precision=lax.Precision.HIGHEST))
        crow = lax.broadcasted_iota(jnp.int32, cand.shape, 0)
        return jnp.where(crow < PEER_NCAND, cand, -jnp.inf)

    for h in range(H):
        for side in range(2):
            q = qt_ref[(2 * h + side) * PEER_HALF:(2 * h + side + 1) * PEER_HALF, :].astype(BF16)
            s_ref[side * H + h] = jnp.dot(keys_ref[h, side], q, preferred_element_type=F32)
    t_all, n_all = _top16_distinct(s_ref[...])
    t_ref[...] = t_all
    n_ref[...] = n_all
    for h in range(H):
        cand_ref[h] = candidates(t_ref[h], t_ref[H + h])
    tc_all, nc_all = _top16_distinct(cand_ref[...])
    tc_ref[...] = tc_all
    nc_ref[...] = nc_all

    def head(h, carry):
        s1 = s_ref[h]
        s2 = s_ref[H + h]

        def finish(cand, chosen, best, lrow_of, rank2):
            z = jnp.sum(jnp.where(chosen, jnp.exp(cand - best), 0.0), axis=0, keepdims=True)
            length = jnp.dot(rowsel_ref[...], chosen.astype(F32), preferred_element_type=F32)
            lrow = jnp.zeros_like(s1)
            for a in range(PEER_TOPK):
                lrow = jnp.where(lrow_of(a), length[a:a + 1, :], lrow)
            lrow_ref[h] = lrow
            rank2_ref[h] = rank2
            e2_ref[h] = jnp.exp(s2 - jnp.max(s2, axis=0, keepdims=True)) / z

        e1_ref[h] = jnp.exp(s1 - jnp.max(s1, axis=0, keepdims=True))
        ties = jnp.max(jnp.abs(n_ref[h] - k) + jnp.abs(n_ref[H + h] - k) + jnp.abs(nc_ref[h] - k))

        @pl.when(ties == 0.0)
        def _():
            t1 = t_ref[h]
            t2 = t_ref[H + h]
            tc = tc_ref[h]
            cand = cand_ref[h]
            rank2 = jnp.full(s2.shape, k, F32)
            for r in range(PEER_TOPK):
                rank2 = jnp.where(s2 == t2[r:r + 1, :], float(r), rank2)
            finish(cand, cand >= tc[PEER_TOPK - 1:PEER_TOPK, :], tc[0:1, :],
                   lambda a: s1 == t1[a:a + 1, :], rank2)

        @pl.when(ties != 0.0)
        def _():
            rank1, v1 = _top16(s1)
            rank2, v2 = _top16(s2)
            cand_exact = candidates(v1, v2)
            crank, cvals = _top16(cand_exact)
            finish(cand_exact, crank < k, cvals[0:1, :], lambda a: rank1 == float(a), rank2)

        return carry

    any_ties = jnp.max(jnp.abs(n_all[:H] - k) + jnp.abs(n_all[H:] - k) + jnp.abs(nc_all - k))

    @pl.when(any_ties == 0.0)
    def _():
        s1 = s_ref[0:H]
        s2 = s_ref[H:2 * H]
        t1 = t_ref[0:H]
        t2 = t_ref[H:2 * H]
        tc = tc_ref[...]
        cand = cand_ref[...]
        chosen = cand >= tc[:, PEER_TOPK - 1:PEER_TOPK, :]
        z = jnp.sum(jnp.where(chosen, jnp.exp(cand - tc[:, 0:1, :]), 0.0), axis=1, keepdims=True)
        chosen_f = chosen.astype(F32)
        length = jnp.stack([jnp.dot(rowsel_ref[...], chosen_f[h], preferred_element_type=F32)
                            for h in range(H)], axis=0)
        lrow = jnp.zeros_like(s1)
        rank2 = jnp.full(s2.shape, k, F32)
        for a in range(PEER_TOPK):
            lrow = jnp.where(s1 == t1[:, a:a + 1, :], length[:, a:a + 1, :], lrow)
            rank2 = jnp.where(s2 == t2[:, a:a + 1, :], float(a), rank2)
        lrow_ref[...] = lrow
        rank2_ref[...] = rank2
        e1_ref[...] = jnp.exp(s1 - t1[:, 0:1, :])
        e2_ref[...] = jnp.exp(s2 - t2[:, 0:1, :]) / z

    @pl.when(any_ties != 0.0)
    def _():
        lax.fori_loop(0, PEER_HEADS, head, 0)


def peer_route(qt, keys_bf16):
    _, t = qt.shape
    bt = LANES
    seta = np.zeros((PEER_NCAND_PAD, PEER_TOPK), np.float32)
    setb = np.zeros((PEER_NCAND_PAD, PEER_TOPK), np.float32)
    rowsel = np.zeros((PEER_TOPK, PEER_NCAND_PAD), np.float32)
    for k, (a, b) in enumerate(PEER_CAND):
        seta[k, a] = 1.0
        setb[k, b] = 1.0
        rowsel[a, k] = 1.0
    table = jax.ShapeDtypeStruct((PEER_HEADS, PEER_NKEYS, t), F32)
    tspec = pl.BlockSpec((PEER_HEADS, PEER_NKEYS, bt), lambda i: (0, 0, i))
    const2 = lambda i: (0, 0)
    return pl.pallas_call(
        _route_kernel,
        out_shape=(table, table, table, table),
        grid=(t // bt,),
        in_specs=[pl.BlockSpec((PEER_HEADS * 2 * PEER_HALF, bt), lambda i: (0, i)),
                  pl.BlockSpec((PEER_HEADS, 2, PEER_NKEYS, PEER_HALF), lambda i: (0, 0, 0, 0)),
                  pl.BlockSpec((PEER_NCAND_PAD, PEER_TOPK), const2),
                  pl.BlockSpec((PEER_NCAND_PAD, PEER_TOPK), const2),
                  pl.BlockSpec((PEER_TOPK, PEER_NCAND_PAD), const2)],
        out_specs=(tspec, tspec, tspec, tspec),
        scratch_shapes=[pltpu.VMEM((2 * PEER_HEADS, PEER_NKEYS, bt), F32),
                        pltpu.VMEM((2 * PEER_HEADS, PEER_TOPK, bt), F32),
                        pltpu.VMEM((2 * PEER_HEADS, 1, bt), F32),
                        pltpu.VMEM((PEER_HEADS, PEER_NCAND_PAD, bt), F32),
                        pltpu.VMEM((PEER_HEADS, PEER_TOPK, bt), F32),
                        pltpu.VMEM((PEER_HEADS, 1, bt), F32)],
        compiler_params=_params(("parallel",)),
        name="peer_route",
    )(qt, keys_bf16, seta, setb, rowsel)


def _peer_act_kernel(u_ref, xt_ref, o_ref):
    hid = jnp.dot(u_ref[...], xt_ref[...], preferred_element_type=F32)
    o_ref[...] = _gelu_tanh(hid).astype(o_ref.dtype)


def peer_act(u_bf16, layer, xt, *, bt=512, be=2048):
    _, e, d = u_bf16.shape
    _, t = xt.shape
    return pl.pallas_call(
        _peer_act_kernel,
        out_shape=jax.ShapeDtypeStruct((e, t), BF16),
        grid=(t // bt, e // be),
        in_specs=[_spec2d(u_bf16, (be, d), lambda i, j: (j, 0), lead=layer),
                  pl.BlockSpec((d, bt), lambda i, j: (0, i))],
        out_specs=pl.BlockSpec((be, bt), lambda i, j: (j, i)),
        compiler_params=_params(("parallel", "arbitrary")),
        name="peer_act",
    )(u_bf16, xt)


def _peer_out_kernel(vt_ref, a_ref, lrow_ref, e1_ref, rank2_ref, e2_ref, o_ref, p_ref, *, nsub):
    e = pl.program_id(1)
    bt = o_ref.shape[1]
    nparts = PEER_NKEYS // BF16_ROWS

    @pl.when(e == 0)
    def _():
        o_ref[...] = jnp.zeros_like(o_ref)

    first_rows = [[(lrow_ref[h, pl.ds(e * nsub + ii, 1), :], e1_ref[h, pl.ds(e * nsub + ii, 1), :])
                   for ii in range(nsub)] for h in range(PEER_HEADS)]
    for tb, first in ((tb, first) for tb in range(bt // LANES) for first in range(0, nsub, PEER_SLAB_GROUP)):
        cols = slice(tb * LANES, (tb + 1) * LANES)
        slabs = range(first, first + PEER_SLAB_GROUP)
        acc = {ii: [None] * nparts for ii in slabs}
        for h in range(PEER_HEADS):
            bcast = lambda row: jnp.broadcast_to(row[:, cols], (BF16_ROWS, LANES)).astype(BF16)
            length = {ii: bcast(first_rows[h][ii][0]) for ii in slabs}
            gate1 = {ii: bcast(first_rows[h][ii][1]) for ii in slabs}
            for p in range(nparts):
                rows = slice(p * BF16_ROWS, (p + 1) * BF16_ROWS)
                rank2 = rank2_ref[h, rows, cols].astype(BF16)
                gate2 = e2_ref[h, rows, cols].astype(BF16)
                for ii in slabs:
                    contrib = jnp.where(rank2 < length[ii], gate2, jnp.zeros_like(gate2)) * gate1[ii]
                    acc[ii][p] = contrib if acc[ii][p] is None else acc[ii][p] + contrib
        for ii in slabs:
            for p in range(nparts):
                rows = slice(ii * PEER_NKEYS + p * BF16_ROWS, ii * PEER_NKEYS + (p + 1) * BF16_ROWS)
                p_ref[rows, cols] = acc[ii][p] * a_ref[rows, cols]
    o_ref[...] += jnp.dot(vt_ref[...], p_ref[...], preferred_element_type=F32)


def peer_out(v_t, layer, at, tables, *, bt=512, be=512):
    _, d, e = v_t.shape
    _, t = at.shape
    tspec = pl.BlockSpec((PEER_HEADS, PEER_NKEYS, bt), lambda i, j: (0, 0, i))
    return pl.pallas_call(
        functools.partial(_peer_out_kernel, nsub=be // PEER_NKEYS),
        out_shape=jax.ShapeDtypeStruct((d, t), F32),
        grid=(t // bt, e // be),
        in_specs=[_spec2d(v_t, (d, be), lambda i, j: (0, j), lead=layer),
                  pl.BlockSpec((be, bt), lambda i, j: (j, i)),
                  tspec, tspec, tspec, tspec],
        out_specs=pl.BlockSpec((d, bt), lambda i, j: (0, i)),
        scratch_shapes=[pltpu.VMEM((be, bt), BF16)],
        compiler_params=_params(("parallel", "arbitrary")),
        name="peer_out",
    )(v_t, at, *tables)


def peer_ffn_t(xt, layer, wq_t, keys_bf16, u_bf16, v_t):
    qt = matmul(wq_t, xt, bm=1024, bn=1024, name="peer_q")
    tables = peer_route(qt, keys_bf16)
    return peer_out(v_t, layer, peer_act(u_bf16, layer, xt), tables)


def _conv_silu_kernel(x_ref, w_ref, b_ref, o_ref, carry_ref):
    @pl.when(pl.program_id(1) == 0)
    def _():
        carry_ref[...] = jnp.zeros_like(carry_ref)

    x = x_ref[...]
    prev = carry_ref[...]
    acc = x * w_ref[SSM_CONV_K - 1:SSM_CONV_K, :] + b_ref[...]
    for k in range(1, SSM_CONV_K):
        acc = acc + _shift_rows(x, prev, k) * w_ref[SSM_CONV_K - 1 - k:SSM_CONV_K - k, :]
    carry_ref[...] = x[x.shape[0] - 8:]
    o_ref[...] = jax.nn.silu(acc).astype(o_ref.dtype)


def conv_silu(x, w, b, *, bt=1024, bc=1024):
    t, c = x.shape
    bt = min(bt, t)
    return pl.pallas_call(
        _conv_silu_kernel,
        out_shape=jax.ShapeDtypeStruct((t, c), BF16),
        grid=(c // bc, t // bt),
        in_specs=[pl.BlockSpec((bt, bc), lambda j, i: (i, j)),
                  pl.BlockSpec((SSM_CONV_K, bc), lambda j, i: (0, j)),
                  pl.BlockSpec((1, bc), lambda j, i: (0, j))],
        out_specs=pl.BlockSpec((bt, bc), lambda j, i: (i, j)),
        scratch_shapes=[pltpu.VMEM((8, bc), F32)],
        compiler_params=_params(("parallel", "arbitrary")),
        name="conv_silu",
    )(x, w, b.reshape(1, c))


def _dt_prep_kernel(dt_ref, bias_ref, alog_ref, acs_ref, acst_ref, rowp_ref):
    L = SSD_CHUNK
    dt = jax.nn.softplus(dt_ref[...] + bias_ref[...])
    da = dt * (-jnp.exp(alog_ref[...]))
    li = lax.broadcasted_iota(jnp.int32, (L, L), 0)
    si = lax.broadcasted_iota(jnp.int32, (L, L), 1)
    acs = jnp.dot((li >= si).astype(F32), da, preferred_element_type=F32,
                  precision=lax.Precision.HIGHEST) * math.log2(math.e)
    acs_ref[...] = acs
    acst_ref[...] = acs.T
    rowp_ref[...] = (acs - jnp.log2(dt)).T


def dt_prep(dt_raw, dt_bias, a_log):
    t, heads = dt_raw.shape
    L = SSD_CHUNK
    vec = pl.BlockSpec((1, heads), lambda c: (0, 0))
    tr = pl.BlockSpec((heads, L), lambda c: (0, c))
    return pl.pallas_call(
        _dt_prep_kernel,
        out_shape=(jax.ShapeDtypeStruct((t, heads), F32), jax.ShapeDtypeStruct((heads, t), F32),
                   jax.ShapeDtypeStruct((heads, t), F32)),
        grid=(t // L,),
        in_specs=[pl.BlockSpec((L, heads), lambda c: (c, 0)), vec, vec],
        out_specs=(pl.BlockSpec((L, heads), lambda c: (c, 0)), tr, tr),
        compiler_params=_params(("parallel",)),
        name="dt_prep",
    )(dt_raw, dt_bias.reshape(1, heads), a_log.reshape(1, heads))


def _ssd_kernel(x_ref, b_ref, c_ref, acs_ref, acst_ref, rowp_ref,
                dskip_ref, z_ref, g_ref, y_ref, state_ref, rhs_ref, gated_ref):
    L = SSD_CHUNK
    P = SSM_HEADDIM
    R = SSM_HEADS // SSM_GROUPS
    sumsq = None

    @pl.when(pl.program_id(1) == 0)
    def _():
        state_ref[...] = jnp.zeros_like(state_ref)
        rhs_ref[...] = jnp.zeros_like(rhs_ref)

    li = lax.broadcasted_iota(jnp.int32, (L, L), 0)
    si = lax.broadcasted_iota(jnp.int32, (L, L), 1)
    causal = li >= si
    acs2 = acs_ref[0]
    acst2 = acst_ref[0]
    rowp_all = rowp_ref[0]

    bmat = b_ref[...].astype(F32)
    cmat = c_ref[...].astype(F32)
    cb = lax.dot_general(cmat.astype(BF16), bmat.astype(BF16), (((1,), (1,)), ((), ())),
                         preferred_element_type=F32)
    bt = bmat.T

    lane = lax.broadcasted_iota(jnp.int32, (L, LANES), 1)
    low = lane < P
    zeros_b = jnp.zeros((L, LANES), BF16)
    for tile in range(R // 4):
        for pair in range(2):
            cols = slice(tile * 256 + pair * LANES, tile * 256 + (pair + 1) * LANES)
            xb = x_ref[:, cols].astype(BF16)
            sb = state_ref[:, cols].astype(BF16)
            dst = slice(pair * LANES, (pair + 1) * LANES)
            for k, keep in enumerate((low, ~low)):
                base = (2 * pair + k) * L
                rhs_ref[tile, base:base + L, dst] = jnp.where(keep, xb, zeros_b)
                rhs_ref[tile, 4 * L + base:4 * L + base + L, dst] = jnp.where(keep, sb, zeros_b)

    head_of_lane = lax.broadcasted_iota(jnp.int32, (1, R * P), 1) // P
    state_decay = jnp.zeros((1, R * P), F32)
    lhs_y, lhs_state = [], []
    for tile in range(R // 4):
        on_x, on_state, to_state = [], [], []
        for r in range(4 * tile, 4 * tile + 4):
            col = acs2[:, r:r + 1]
            rowp = rowp_all[r:r + 1, :]
            last = acst2[r:r + 1, L - 1:L]
            on_x.append((cb * jnp.exp2(jnp.where(causal, col - rowp, -jnp.inf))).astype(BF16))
            on_state.append((cmat * jnp.exp2(col)).astype(BF16))
            to_state.append((bt * jnp.exp2(last - rowp)).astype(BF16))
            state_decay = jnp.where(head_of_lane == r, jnp.exp2(last), state_decay)
        lhs_y.append(jnp.concatenate(on_x + on_state, axis=1))
        lhs_state.append(jnp.concatenate(to_state, axis=1))
    y_mix = [jnp.dot(lhs_y[tile], rhs_ref[tile], preferred_element_type=F32)
             for tile in range(R // 4)]
    new_state = [jnp.dot(lhs_state[tile], rhs_ref[tile, 0:4 * L, :], preferred_element_type=F32)
                 for tile in range(R // 4)]
    for tile in range(R // 4):
        cols = slice(tile * 256, (tile + 1) * 256)
        y = y_mix[tile] + dskip_ref[0][:, cols] * x_ref[:, cols].astype(F32)
        gated = y * jax.nn.silu(z_ref[:, cols])
        gated_ref[:, cols] = gated
        part = jnp.sum(gated * gated, axis=-1, keepdims=True)
        sumsq = part if sumsq is None else sumsq + part
    for tile in range(R // 4):
        cols = slice(tile * 256, (tile + 1) * 256)
        state_ref[:, cols] = state_ref[:, cols] * state_decay[:, cols] + new_state[tile]
    scale = lax.rsqrt(sumsq * (1.0 / (R * P)) + EPS)
    y_ref[...] = (gated_ref[...] * scale * g_ref[...]).astype(y_ref.dtype)


def ssd(xbc, dt_raw, dt_bias, a_log, d_skip, z, norm_g):
    t = xbc.shape[0]
    G, R, L = SSM_GROUPS, SSM_HEADS // SSM_GROUPS, SSD_CHUNK
    gw = R * SSM_HEADDIM
    acs, acst, rowp = dt_prep(dt_raw, dt_bias, a_log)
    acs = acs.reshape(t, G, R).transpose(1, 0, 2)
    acst = acst.reshape(G, R, t)
    rowp = rowp.reshape(G, R, t)
    dskip = jnp.repeat(d_skip, SSM_HEADDIM).reshape(G, 1, gw)
    per_time = pl.BlockSpec((1, R, L), lambda g, c: (g, 0, c))
    nb = SSM_INNER // SSM_STATE
    return pl.pallas_call(
        _ssd_kernel,
        out_shape=jax.ShapeDtypeStruct((t, SSM_INNER), BF16),
        grid=(G, t // L),
        in_specs=[pl.BlockSpec((L, gw), lambda g, c: (c, g)),
                  pl.BlockSpec((L, SSM_STATE), lambda g, c: (c, nb + g)),
                  pl.BlockSpec((L, SSM_STATE), lambda g, c: (c, nb + G + g)),
                  pl.BlockSpec((1, L, R), lambda g, c: (g, c, 0)), per_time, per_time,
                  pl.BlockSpec((1, 1, gw), lambda g, c: (g, 0, 0)),
                  pl.BlockSpec((L, gw), lambda g, c: (c, g)),
                  pl.BlockSpec((1, gw), lambda g, c: (0, g))],
        out_specs=pl.BlockSpec((L, gw), lambda g, c: (c, g)),
        scratch_shapes=[pltpu.VMEM((SSM_STATE, gw), F32),
                        pltpu.VMEM((R // 4, 4 * 2 * L, 256), BF16),
                        pltpu.VMEM((L, gw), F32)],
        compiler_params=_params(("parallel", "arbitrary")),
        name="ssd",
    )(xbc, xbc, xbc, acs, acst, rowp, dskip, z, norm_g.reshape(1, SSM_INNER))


def kernel(x, mix_norm, ffn_norm, final_norm, ev_w_in, ev_conv_w, ev_sgu_norm, ev_sgu_w, ev_sgu_b, ev_w_out, od_w_in, od_conv_w, od_conv_b, od_dt_bias, od_a_log, od_d, od_norm, od_w_out, peer_wq, peer_keys, peer_u, peer_v):
    h = x

    u_bf16 = peer_u.astype(BF16)
    v_t = jnp.swapaxes(peer_v, 1, 2).astype(BF16)

    def peer(xt, i):
        return peer_ffn_t(xt, i, peer_wq[i].T.astype(BF16), peer_keys[i].astype(BF16), u_bf16, v_t)

    hn = rmsnorm(h, mix_norm[0], transpose_out=False)
    proj = matmul_wcast(hn, ev_w_in, 0, ev_w_in.shape[2], bm=1024, bn=1024, name="ev_in")
    ycat = evmix(proj, ev_conv_w[0], ev_sgu_norm[0], ev_sgu_w[0], ev_sgu_b[0])
    h = matmul(ycat, ev_w_out[0].astype(BF16), bm=1024, bn=1024, res=h, name="ev_out")
    delta_t = peer(rmsnorm(h, ffn_norm[0], transpose_out=True), 0)

    h, hn = resid_norm(h, delta_t, mix_norm[1], want_h=True, norm_dtype=BF16)
    w_in = od_w_in
    z = matmul_wcast(hn, w_in, 0, SSM_INNER, bm=1024, bn=1024, name="od_in_z")
    xbc = matmul_wcast(hn, w_in, SSM_INNER, SSM_CONV_DIM, bm=1024, bn=1024, name="od_in_xbc")
    dt_raw = matmul_wcast(hn, w_in, SSM_INNER + SSM_CONV_DIM, SSM_HEADS, bm=1024, bn=SSM_HEADS,
                          name="od_in_dt")
    xbc = conv_silu(xbc, od_conv_w[0], od_conv_b[0])
    yn = ssd(xbc, dt_raw, od_dt_bias[0], od_a_log[0], od_d[0], z, od_norm[0])
    h = matmul(yn, od_w_out[0].astype(BF16), bm=1024, bn=1024, bk=2048, res=h, name="od_out")
    delta_t = peer(rmsnorm(h, ffn_norm[1], transpose_out=True), 1)

    out = resid_norm(h, delta_t, final_norm, want_h=False, norm_dtype=F32)
    return out
```

```python
import functools
import math

import jax
import jax.numpy as jnp
import numpy as np
from jax import lax
from jax.experimental import pallas as pl
from jax.experimental.pallas import tpu as pltpu

F32 = jnp.float32
BF16 = jnp.bfloat16

EPS = 1e-6
LANES = 128
BF16_ROWS = 16
VMEM_LIMIT = 56 * 1024 * 1024

D_MODEL = 4096
CONV_WIDTH = 2048
CONV_K = 3
SGU_WIDTH = 2048
SGU_HEADS = 16
SGU_BLOCK = 128
SSM_INNER = 8192
SSM_HEADDIM = 64
SSM_HEADS = 128
SSM_GROUPS = 8
SSM_STATE = 128
SSM_CONV_K = 4
SSM_CONV_DIM = SSM_INNER + 2 * SSM_GROUPS * SSM_STATE
SSD_CHUNK = 128
PEER_HEADS = 8
PEER_NKEYS = 128
PEER_HALF = 128
PEER_TOPK = 16
PEER_CAND = [(a, b) for a in range(PEER_TOPK) for b in range(PEER_TOPK)
             if (a + 1) * (b + 1) <= PEER_TOPK]
PEER_NCAND = len(PEER_CAND)
PEER_NCAND_PAD = -(-PEER_NCAND // 8) * 8
PEER_SLAB_GROUP = 4


def _spec2d(arr, block, index_map, lead=0, **kwargs):
    if arr.ndim == 2:
        return pl.BlockSpec(block, index_map, **kwargs)
    assert arr.ndim == 3 and lead < arr.shape[0]
    return pl.BlockSpec((None,) + tuple(block), lambda *idx: (lead,) + tuple(index_map(*idx)), **kwargs)


def _params(sem):
    return pltpu.CompilerParams(dimension_semantics=sem, vmem_limit_bytes=VMEM_LIMIT)


def _mm_kernel(*refs, nk, has_res):
    if has_res:
        a_ref, b_ref, r_ref, o_ref = refs
    else:
        a_ref, b_ref, o_ref = refs
        r_ref = None

    if nk == 1:
        part = jnp.dot(a_ref[...], b_ref[...], preferred_element_type=F32)
        if has_res:
            part = part + r_ref[...]
        o_ref[...] = part.astype(o_ref.dtype)
    else:
        @pl.when(pl.program_id(2) == 0)
        def _():
            o_ref[...] = r_ref[...] if has_res else jnp.zeros_like(o_ref)

        o_ref[...] += jnp.dot(a_ref[...], b_ref[...], preferred_element_type=F32)


def matmul(a, b, *, bm, bn, bk=None, out_dtype=F32, res=None, name="mm"):
    m, kdim = a.shape
    _, n = b.shape
    bk = kdim if bk is None else bk
    bm, bn = min(bm, m), min(bn, n)
    nk = kdim // bk
    assert m % bm == 0 and n % bn == 0 and kdim % bk == 0
    assert nk == 1 or out_dtype == F32
    in_specs = [pl.BlockSpec((bm, bk), lambda i, j, k: (i, k)),
                pl.BlockSpec((bk, bn), lambda i, j, k: (k, j))]
    args = [a, b]
    if res is not None:
        in_specs.append(_spec2d(res, (bm, bn), lambda i, j, k: (i, j)))
        args.append(res)
    return pl.pallas_call(
        functools.partial(_mm_kernel, nk=nk, has_res=res is not None),
        out_shape=jax.ShapeDtypeStruct((m, n), out_dtype),
        grid=(m // bm, n // bn, nk),
        in_specs=in_specs,
        out_specs=pl.BlockSpec((bm, bn), lambda i, j, k: (i, j)),
        compiler_params=_params(("parallel", "parallel", "arbitrary")),
        name=name,
    )(*args)


def _mm_wcast_kernel(a_ref, w_ref, o_ref, wb_ref):
    @pl.when(pl.program_id(1) == 0)
    def _():
        wb_ref[...] = w_ref[...].astype(BF16)

    o_ref[...] = jnp.dot(a_ref[...], wb_ref[...], preferred_element_type=F32).astype(o_ref.dtype)


def matmul_wcast(a, w, col0, ncols, *, bm, bn, out_dtype=F32, name="mm_wcast"):
    m, kdim = a.shape
    bm, bn = min(bm, m), min(bn, ncols)
    assert m % bm == 0 and ncols % bn == 0 and col0 % bn == 0 and w.shape[-2] == kdim
    w_spec = _spec2d(w, (kdim, bn), lambda j, i: (0, col0 // bn + j), pipeline_mode=pl.Buffered(1))
    return pl.pallas_call(
        _mm_wcast_kernel,
        out_shape=jax.ShapeDtypeStruct((m, ncols), out_dtype),
        grid=(ncols // bn, m // bm),
        in_specs=[pl.BlockSpec((bm, kdim), lambda j, i: (i, 0)), w_spec],
        out_specs=pl.BlockSpec((bm, bn), lambda j, i: (i, j)),
        scratch_shapes=[pltpu.VMEM((kdim, bn), BF16)],
        compiler_params=_params(("parallel", "arbitrary")),
        name=name,
    )(a, w)


_GELU_K0 = -2.0 * math.sqrt(2.0 / math.pi) * math.log2(math.e)
_GELU_K1 = _GELU_K0 * 0.044715


def _gelu_tanh(x):
    return x / (1.0 + jnp.exp2(x * (_GELU_K0 + _GELU_K1 * (x * x))))


def _rms(x, g):
    return x * lax.rsqrt(jnp.mean(x * x, axis=-1, keepdims=True) + EPS) * g


def _rmsnorm_kernel(x_ref, g_ref, o_ref, *, transpose_out):
    y = _rms(x_ref[...], g_ref[...])
    if transpose_out:
        y = y.T
    o_ref[...] = y.astype(o_ref.dtype)


def rmsnorm(x, g, *, transpose_out, bt=256):
    t, d = x.shape[-2:]
    if transpose_out:
        out_shape, out_spec = (d, t), pl.BlockSpec((d, bt), lambda i: (0, i))
    else:
        out_shape, out_spec = (t, d), pl.BlockSpec((bt, d), lambda i: (i, 0))
    return pl.pallas_call(
        functools.partial(_rmsnorm_kernel, transpose_out=transpose_out),
        out_shape=jax.ShapeDtypeStruct(out_shape, BF16),
        grid=(t // bt,),
        in_specs=[_spec2d(x, (bt, d), lambda i: (i, 0)),
                  pl.BlockSpec((1, d), lambda i: (0, 0))],
        out_specs=out_spec,
        compiler_params=_params(("parallel",)),
        name="rmsnorm_t" if transpose_out else "rmsnorm",
    )(x, g.reshape(1, d))


def _resid_norm_kernel(h_ref, dt_ref, g_ref, *o_refs, want_h):
    h = h_ref[...] + dt_ref[...].T
    if want_h:
        hn_ref, n_ref = o_refs
        hn_ref[...] = h
    else:
        n_ref, = o_refs
    n_ref[...] = _rms(h, g_ref[...]).astype(n_ref.dtype)


def resid_norm(h, delta_t, g, *, want_h, norm_dtype, bt=256):
    t, d = h.shape
    row = pl.BlockSpec((bt, d), lambda i: (i, 0))
    norm_shape = jax.ShapeDtypeStruct((t, d), norm_dtype)
    if want_h:
        out_shape, out_specs = (jax.ShapeDtypeStruct((t, d), F32), norm_shape), (row, row)
    else:
        out_shape = jax.ShapeDtypeStruct((1, t, d), norm_dtype)
        out_specs = pl.BlockSpec((None, bt, d), lambda i: (0, i, 0))
    return pl.pallas_call(
        functools.partial(_resid_norm_kernel, want_h=want_h),
        out_shape=out_shape,
        grid=(t // bt,),
        in_specs=[row, pl.BlockSpec((d, bt), lambda i: (0, i)),
                  pl.BlockSpec((1, d), lambda i: (0, 0))],
        out_specs=out_specs,
        compiler_params=_params(("parallel",)),
        name="resid_norm",
    )(h, delta_t, g.reshape(1, d))


def _shift_rows(p, prev, k):
    rolled = pltpu.roll(p, k, axis=0)
    head = pltpu.roll(prev, k, axis=0)
    rows = lax.broadcasted_iota(jnp.int32, (8, p.shape[1]), 0)
    fixed = jnp.where(rows < k, head, rolled[:8])
    return jnp.concatenate([fixed, rolled[8:]], axis=0)


def _evmix_kernel(proj_ref, cw_ref, ng_ref, sw_ref, sbt_ref, o_ref, carry_ref):
    @pl.when(pl.program_id(0) == 0)
    def _():
        carry_ref[...] = jnp.zeros_like(carry_ref)

    c = CONV_WIDTH
    gb = proj_ref[:, 0:c]
    p = proj_ref[:, c:2 * c] * proj_ref[:, 2 * c:3 * c]
    prev = carry_ref[...]
    conv = p * cw_ref[CONV_K - 1:CONV_K, :]
    for k in range(1, CONV_K):
        conv = conv + _shift_rows(p, prev, k) * cw_ref[CONV_K - 1 - k:CONV_K - k, :]
    carry_ref[...] = p[p.shape[0] - 8:]
    o_ref[:, 0:c] = (gb * conv).astype(o_ref.dtype)

    u = _gelu_tanh(proj_ref[:, 3 * c:3 * c + SGU_WIDTH])
    v = _gelu_tanh(proj_ref[:, 3 * c + SGU_WIDTH:3 * c + 2 * SGU_WIDTH])
    mu = jnp.mean(v, axis=-1, keepdims=True)
    vc = v - mu
    var = jnp.mean(vc * vc, axis=-1, keepdims=True)
    vn = (vc * lax.rsqrt(var + EPS) * ng_ref[...]).astype(BF16)
    q_idx = lax.broadcasted_iota(jnp.int32, (SGU_BLOCK, SGU_BLOCK), 0)
    p_idx = lax.broadcasted_iota(jnp.int32, (SGU_BLOCK, SGU_BLOCK), 1)
    hw = SGU_WIDTH // SGU_HEADS
    for g in range(SGU_HEADS):
        ws = jnp.where(q_idx >= p_idx, sw_ref[g], 0.0).astype(BF16)
        sv = jnp.dot(ws, vn[:, g * hw:(g + 1) * hw], preferred_element_type=F32)
        sv = sv + sbt_ref[:, g:g + 1]
        o_ref[:, c + g * hw:c + (g + 1) * hw] = (u[:, g * hw:(g + 1) * hw] * sv).astype(o_ref.dtype)


def evmix(proj, conv_w, sgu_norm, sgu_w, sgu_b):
    t, width = proj.shape
    bt = SGU_BLOCK
    return pl.pallas_call(
        _evmix_kernel,
        out_shape=jax.ShapeDtypeStruct((t, CONV_WIDTH + SGU_WIDTH), BF16),
        grid=(t // bt,),
        in_specs=[pl.BlockSpec((bt, width), lambda i: (i, 0)),
                  pl.BlockSpec((CONV_K, CONV_WIDTH), lambda i: (0, 0)),
                  pl.BlockSpec((1, SGU_WIDTH), lambda i: (0, 0)),
                  pl.BlockSpec((SGU_HEADS, SGU_BLOCK, SGU_BLOCK), lambda i: (0, 0, 0)),
                  pl.BlockSpec((SGU_BLOCK, SGU_HEADS), lambda i: (0, 0))],
        out_specs=pl.BlockSpec((bt, CONV_WIDTH + SGU_WIDTH), lambda i: (i, 0)),
        scratch_shapes=[pltpu.VMEM((8, CONV_WIDTH), F32)],
        compiler_params=_params(("arbitrary",)),
        name="evmix",
    )(proj, conv_w, sgu_norm.reshape(1, SGU_WIDTH), sgu_w, jnp.transpose(sgu_b))


def _top16(s):
    n, width = s.shape
    rows = lax.broadcasted_iota(jnp.int32, s.shape, 0).astype(F32)
    slot = lax.broadcasted_iota(jnp.int32, (PEER_TOPK, width), 0)
    cur = s
    rank = jnp.full(s.shape, float(PEER_TOPK), F32)
    vals = jnp.zeros((PEER_TOPK, width), F32)
    for r in range(PEER_TOPK):
        m = jnp.max(cur, axis=0, keepdims=True)
        first = jnp.min(jnp.where(cur == m, rows, float(n)), axis=0, keepdims=True)
        sel = rows == first
        rank = jnp.where(sel, float(r), rank)
        cur = jnp.where(sel, -jnp.inf, cur)
        vals = jnp.where(slot == r, m, vals)
    return rank, vals


def _top16_distinct(s):
    ax = s.ndim - 2
    slot = lax.broadcasted_iota(jnp.int32, s.shape[:ax] + (PEER_TOPK, s.shape[-1]), ax)
    cur = s
    vals = jnp.zeros(slot.shape, F32)
    for r in range(PEER_TOPK):
        m = jnp.max(cur, axis=ax, keepdims=True)
        cur = jnp.where(cur == m, -jnp.inf, cur)
        vals = jnp.where(slot == r, m, vals)
    last = vals[..., PEER_TOPK - 1:PEER_TOPK, :]
    count = jnp.sum(jnp.where(s >= last, 1.0, 0.0), axis=ax, keepdims=True)
    return vals, count


def _route_kernel(qt_ref, keys_ref, seta_ref, setb_ref, rowsel_ref,
                  lrow_ref, e1_ref, rank2_ref, e2_ref, s_ref, t_ref, cand_ref, tc_ref):
    H = PEER_HEADS
    k = float(PEER_TOPK)

    def finish(h, s1, s2, cand, chosen, is_first_rank, rank2):
        z = jnp.sum(jnp.where(chosen, jnp.exp(cand - jnp.max(cand, axis=0, keepdims=True)), 0.0),
                    axis=0, keepdims=True)
        length = jnp.dot(rowsel_ref[...], chosen.astype(F32), preferred_element_type=F32)
        lrow = jnp.zeros_like(s1)
        for a in range(PEER_TOPK):
            lrow = jnp.where(is_first_rank(a), length[a:a + 1, :], lrow)
        lrow_ref[h] = lrow
        rank2_ref[h] = rank2
        e1_ref[h] = jnp.exp(s1 - jnp.max(s1, axis=0, keepdims=True))
        e2_ref[h] = jnp.exp(s2 - jnp.max(s2, axis=0, keepdims=True)) / z

    def candidates(v1, v2):
        cand = (jnp.dot(seta_ref[...], v1, preferred_element_type=F32, precision=lax.Precision.HIGHEST)
                + jnp.dot(setb_ref[...], v2, preferred_element_type=F32, precision=lax.Precision.HIGHEST))
        crow = lax.broadcasted_iota(jnp.int32, cand.shape, 0)
        return jnp.where(crow < PEER_NCAND, cand, -jnp.inf)

    for h in range(H):
        for side in range(2):
            q = qt_ref[(2 * h + side) * PEER_HALF:(2 * h + side + 1) * PEER_HALF, :].astype(BF16)
            s_ref[side * H + h] = jnp.dot(keys_ref[h, side], q, preferred_element_type=F32)
    t_all, n_all = _top16_distinct(s_ref[...])
    t_ref[...] = t_all
    for h in range(H):
        cand_ref[h] = candidates(t_ref[h], t_ref[H + h])
    tc_all, nc_all = _top16_distinct(cand_ref[...])
    tc_ref[...] = tc_all
    any_ties = jnp.max(jnp.abs(n_all[:H] - k) + jnp.abs(n_all[H:] - k) + jnp.abs(nc_all - k))

    @pl.when(any_ties == 0.0)
    def _():
        for h in range(H):
            s1 = s_ref[h]
            s2 = s_ref[H + h]
            t1 = t_ref[h]
            t2 = t_ref[H + h]
            cand = cand_ref[h]
            rank2 = jnp.full(s2.shape, k, F32)
            for r in range(PEER_TOPK):
                rank2 = jnp.where(s2 == t2[r:r + 1, :], float(r), rank2)
            finish(h, s1, s2, cand, cand >= tc_ref[h][PEER_TOPK - 1:PEER_TOPK, :],
                   lambda a, s1=s1, t1=t1: s1 == t1[a:a + 1, :], rank2)

    @pl.when(any_ties != 0.0)
    def _():
        def head(h, carry):
            s1 = s_ref[h]
            s2 = s_ref[H + h]
            rank1, v1 = _top16(s1)
            rank2, v2 = _top16(s2)
            cand = candidates(v1, v2)
            crank, _ = _top16(cand)
            finish(h, s1, s2, cand, crank < k, lambda a: rank1 == float(a), rank2)
            return carry

        lax.fori_loop(0, H, head, 0)


def peer_route(qt, keys_bf16):
    _, t = qt.shape
    bt = LANES
    seta = np.zeros((PEER_NCAND_PAD, PEER_TOPK), np.float32)
    setb = np.zeros((PEER_NCAND_PAD, PEER_TOPK), np.float32)
    rowsel = np.zeros((PEER_TOPK, PEER_NCAND_PAD), np.float32)
    for k, (a, b) in enumerate(PEER_CAND):
        seta[k, a] = 1.0
        setb[k, b] = 1.0
        rowsel[a, k] = 1.0
    table = jax.ShapeDtypeStruct((PEER_HEADS, PEER_NKEYS, t), F32)
    tspec = pl.BlockSpec((PEER_HEADS, PEER_NKEYS, bt), lambda i: (0, 0, i))
    const2 = lambda i: (0, 0)
    return pl.pallas_call(
        _route_kernel,
        out_shape=(table, table, table, table),
        grid=(t // bt,),
        in_specs=[pl.BlockSpec((PEER_HEADS * 2 * PEER_HALF, bt), lambda i: (0, i)),
                  pl.BlockSpec((PEER_HEADS, 2, PEER_NKEYS, PEER_HALF), lambda i: (0, 0, 0, 0)),
                  pl.BlockSpec((PEER_NCAND_PAD, PEER_TOPK), const2),
                  pl.BlockSpec((PEER_NCAND_PAD, PEER_TOPK), const2),
                  pl.BlockSpec((PEER_TOPK, PEER_NCAND_PAD), const2)],
        out_specs=(tspec, tspec, tspec, tspec),
        scratch_shapes=[pltpu.VMEM((2 * PEER_HEADS, PEER_NKEYS, bt), F32),
                        pltpu.VMEM((2 * PEER_HEADS, PEER_TOPK, bt), F32),
                        pltpu.VMEM((PEER_HEADS, PEER_NCAND_PAD, bt), F32),
                        pltpu.VMEM((PEER_HEADS, PEER_TOPK, bt), F32)],
        compiler_params=_params(("parallel",)),
        name="peer_route",
    )(qt, keys_bf16, seta, setb, rowsel)


def _peer_act_kernel(u_ref, xt_ref, o_ref):
    hid = jnp.dot(u_ref[...], xt_ref[...], preferred_element_type=F32)
    o_ref[...] = _gelu_tanh(hid).astype(o_ref.dtype)


def peer_act(u_bf16, layer, xt, *, bt=512, be=2048):
    _, e, d = u_bf16.shape
    _, t = xt.shape
    return pl.pallas_call(
        _peer_act_kernel,
        out_shape=jax.ShapeDtypeStruct((e, t), BF16),
        grid=(t // bt, e // be),
        in_specs=[_spec2d(u_bf16, (be, d), lambda i, j: (j, 0), lead=layer),
                  pl.BlockSpec((d, bt), lambda i, j: (0, i))],
        out_specs=pl.BlockSpec((be, bt), lambda i, j: (j, i)),
        compiler_params=_params(("parallel", "arbitrary")),
        name="peer_act",
    )(u_bf16, xt)


def _peer_out_kernel(vt_ref, a_ref, lrow_ref, e1_ref, rank2_ref, e2_ref, o_ref, p_ref, *, nsub):
    e = pl.program_id(1)
    bt = o_ref.shape[1]
    nparts = PEER_NKEYS // BF16_ROWS

    @pl.when(e == 0)
    def _():
        o_ref[...] = jnp.zeros_like(o_ref)

    first_rows = [[(lrow_ref[h, pl.ds(e * nsub + ii, 1), :], e1_ref[h, pl.ds(e * nsub + ii, 1), :])
                   for ii in range(nsub)] for h in range(PEER_HEADS)]
    for tb, first in ((tb, first) for tb in range(bt // LANES) for first in range(0, nsub, PEER_SLAB_GROUP)):
        cols = slice(tb * LANES, (tb + 1) * LANES)
        slabs = range(first, first + PEER_SLAB_GROUP)
        acc = {ii: [None] * nparts for ii in slabs}
        for h in range(PEER_HEADS):
            bcast = lambda row: jnp.broadcast_to(row[:, cols], (BF16_ROWS, LANES)).astype(BF16)
            length = {ii: bcast(first_rows[h][ii][0]) for ii in slabs}
            gate1 = {ii: bcast(first_rows[h][ii][1]) for ii in slabs}
            for p in range(nparts):
                rows = slice(p * BF16_ROWS, (p + 1) * BF16_ROWS)
                rank2 = rank2_ref[h, rows, cols].astype(BF16)
                gate2 = e2_ref[h, rows, cols].astype(BF16)
                for ii in slabs:
                    contrib = jnp.where(rank2 < length[ii], gate2, jnp.zeros_like(gate2)) * gate1[ii]
                    acc[ii][p] = contrib if acc[ii][p] is None else acc[ii][p] + contrib
        for ii in slabs:
            for p in range(nparts):
                rows = slice(ii * PEER_NKEYS + p * BF16_ROWS, ii * PEER_NKEYS + (p + 1) * BF16_ROWS)
                p_ref[rows, cols] = acc[ii][p] * a_ref[rows, cols]
    o_ref[...] += jnp.dot(vt_ref[...], p_ref[...], preferred_element_type=F32)


def peer_out(v_t, layer, at, tables, *, bt=512, be=512):
    _, d, e = v_t.shape
    _, t = at.shape
    tspec = pl.BlockSpec((PEER_HEADS, PEER_NKEYS, bt), lambda i, j: (0, 0, i))
    return pl.pallas_call(
        functools.partial(_peer_out_kernel, nsub=be // PEER_NKEYS),
        out_shape=jax.ShapeDtypeStruct((d, t), F32),
        grid=(t // bt, e // be),
        in_specs=[_spec2d(v_t, (d, be), lambda i, j: (0, j), lead=layer),
                  pl.BlockSpec((be, bt), lambda i, j: (j, i)),
                  tspec, tspec, tspec, tspec],
        out_specs=pl.BlockSpec((d, bt), lambda i, j: (0, i)),
        scratch_shapes=[pltpu.VMEM((be, bt), BF16)],
        compiler_params=_params(("parallel", "arbitrary")),
        name="peer_out",
    )(v_t, at, *tables)


def peer_ffn_t(xt, layer, wq_t, keys_bf16, u_bf16, v_t):
    qt = matmul(wq_t, xt, bm=1024, bn=1024, name="peer_q")
    tables = peer_route(qt, keys_bf16)
    return peer_out(v_t, layer, peer_act(u_bf16, layer, xt), tables)


def _conv_silu_kernel(x_ref, w_ref, b_ref, o_ref, carry_ref):
    @pl.when(pl.program_id(1) == 0)
    def _():
        carry_ref[...] = jnp.zeros_like(carry_ref)

    x = x_ref[...]
    prev = carry_ref[...]
    acc = x * w_ref[SSM_CONV_K - 1:SSM_CONV_K, :] + b_ref[...]
    for k in range(1, SSM_CONV_K):
        acc = acc + _shift_rows(x, prev, k) * w_ref[SSM_CONV_K - 1 - k:SSM_CONV_K - k, :]
    carry_ref[...] = x[x.shape[0] - 8:]
    o_ref[...] = jax.nn.silu(acc).astype(o_ref.dtype)


def conv_silu(x, w, b, *, bt=1024, bc=1024):
    t, c = x.shape
    bt = min(bt, t)
    return pl.pallas_call(
        _conv_silu_kernel,
        out_shape=jax.ShapeDtypeStruct((t, c), BF16),
        grid=(c // bc, t // bt),
        in_specs=[pl.BlockSpec((bt, bc), lambda j, i: (i, j)),
                  pl.BlockSpec((SSM_CONV_K, bc), lambda j, i: (0, j)),
                  pl.BlockSpec((1, bc), lambda j, i: (0, j))],
        out_specs=pl.BlockSpec((bt, bc), lambda j, i: (i, j)),
        scratch_shapes=[pltpu.VMEM((8, bc), F32)],
        compiler_params=_params(("parallel", "arbitrary")),
        name="conv_silu",
    )(x, w, b.reshape(1, c))


def _dt_prep_kernel(dt_ref, bias_ref, alog_ref, acs_ref, acst_ref, rowp_ref):
    L = SSD_CHUNK
    dt = jax.nn.softplus(dt_ref[...] + bias_ref[...])
    da = dt * (-jnp.exp(alog_ref[...]))
    li = lax.broadcasted_iota(jnp.int32, (L, L), 0)
    si = lax.broadcasted_iota(jnp.int32, (L, L), 1)
    acs = jnp.dot((li >= si).astype(F32), da, preferred_element_type=F32,
                  precision=lax.Precision.HIGHEST) * math.log2(math.e)
    acs_ref[...] = acs
    acst_ref[...] = acs.T
    rowp_ref[...] = (acs - jnp.log2(dt)).T


def dt_prep(dt_raw, dt_bias, a_log):
    t, heads = dt_raw.shape
    L = SSD_CHUNK
    vec = pl.BlockSpec((1, heads), lambda c: (0, 0))
    tr = pl.BlockSpec((heads, L), lambda c: (0, c))
    return pl.pallas_call(
        _dt_prep_kernel,
        out_shape=(jax.ShapeDtypeStruct((t, heads), F32), jax.ShapeDtypeStruct((heads, t), F32),
                   jax.ShapeDtypeStruct((heads, t), F32)),
        grid=(t // L,),
        in_specs=[pl.BlockSpec((L, heads), lambda c: (c, 0)), vec, vec],
        out_specs=(pl.BlockSpec((L, heads), lambda c: (c, 0)), tr, tr),
        compiler_params=_params(("parallel",)),
        name="dt_prep",
    )(dt_raw, dt_bias.reshape(1, heads), a_log.reshape(1, heads))


def _ssd_kernel(x_ref, b_ref, c_ref, acs_ref, acst_ref, rowp_ref,
                dskip_ref, z_ref, g_ref, y_ref, state_ref, rhs_ref, gated_ref):
    L = SSD_CHUNK
    P = SSM_HEADDIM
    R = SSM_HEADS // SSM_GROUPS
    sumsq = None

    @pl.when(pl.program_id(1) == 0)
    def _():
        state_ref[...] = jnp.zeros_like(state_ref)
        rhs_ref[...] = jnp.zeros_like(rhs_ref)

    li = lax.broadcasted_iota(jnp.int32, (L, L), 0)
    si = lax.broadcasted_iota(jnp.int32, (L, L), 1)
    causal = li >= si
    acs2 = acs_ref[0]
    acst2 = acst_ref[0]
    rowp_all = rowp_ref[0]

    bmat = b_ref[...].astype(F32)
    cmat = c_ref[...].astype(F32)
    cb = lax.dot_general(cmat.astype(BF16), bmat.astype(BF16), (((1,), (1,)), ((), ())),
                         preferred_element_type=F32)
    bt = bmat.T

    lane = lax.broadcasted_iota(jnp.int32, (L, LANES), 1)
    low = lane < P
    zeros_b = jnp.zeros((L, LANES), BF16)
    for tile in range(R // 4):
        for pair in range(2):
            cols = slice(tile * 256 + pair * LANES, tile * 256 + (pair + 1) * LANES)
            xb = x_ref[:, cols].astype(BF16)
            sb = state_ref[:, cols].astype(BF16)
            dst = slice(pair * LANES, (pair + 1) * LANES)
            for k, keep in enumerate((low, ~low)):
                base = (2 * pair + k) * L
                rhs_ref[tile, base:base + L, dst] = jnp.where(keep, xb, zeros_b)
                rhs_ref[tile, 4 * L + base:4 * L + base + L, dst] = jnp.where(keep, sb, zeros_b)

    head_of_lane = lax.broadcasted_iota(jnp.int32, (1, R * P), 1) // P
    state_decay = jnp.zeros((1, R * P), F32)
    lhs_y, lhs_state = [], []
    for tile in range(R // 4):
        on_x, on_state, to_state = [], [], []
        for r in range(4 * tile, 4 * tile + 4):
            col = acs2[:, r:r + 1]
            rowp = rowp_all[r:r + 1, :]
            last = acst2[r:r + 1, L - 1:L]
            on_x.append((cb * jnp.exp2(jnp.where(causal, col - rowp, -jnp.inf))).astype(BF16))
            on_state.append((cmat * jnp.exp2(col)).astype(BF16))
            to_state.append((bt * jnp.exp2(last - rowp)).astype(BF16))
            state_decay = jnp.where(head_of_lane == r, jnp.exp2(last), state_decay)
        lhs_y.append(jnp.concatenate(on_x + on_state, axis=1))
        lhs_state.append(jnp.concatenate(to_state, axis=1))
    y_mix = [jnp.dot(lhs_y[tile], rhs_ref[tile], preferred_element_type=F32)
             for tile in range(R // 4)]
    new_state = [jnp.dot(lhs_state[tile], rhs_ref[tile, 0:4 * L, :], preferred_element_type=F32)
                 for tile in range(R // 4)]
    for tile in range(R // 4):
        cols = slice(tile * 256, (tile + 1) * 256)
        y = y_mix[tile] + dskip_ref[0][:, cols] * x_ref[:, cols].astype(F32)
        gated = y * jax.nn.silu(z_ref[:, cols])
        gated_ref[:, cols] = gated
        part = jnp.sum(gated * gated, axis=-1, keepdims=True)
        sumsq = part if sumsq is None else sumsq + part
    for tile in range(R // 4):
        cols = slice(tile * 256, (tile + 1) * 256)
        state_ref[:, cols] = state_ref[:, cols] * state_decay[:, cols] + new_state[tile]
    scale = lax.rsqrt(sumsq * (1.0 / (R * P)) + EPS)
    y_ref[...] = (gated_ref[...] * scale * g_ref[...]).astype(y_ref.dtype)


def ssd(xbc, dt_raw, dt_bias, a_log, d_skip, z, norm_g):
    t = xbc.shape[0]
    G, R, L = SSM_GROUPS, SSM_HEADS // SSM_GROUPS, SSD_CHUNK
    gw = R * SSM_HEADDIM
    acs, acst, rowp = dt_prep(dt_raw, dt_bias, a_log)
    acs = acs.reshape(t, G, R).transpose(1, 0, 2)
    acst = acst.reshape(G, R, t)
    rowp = rowp.reshape(G, R, t)
    dskip = jnp.repeat(d_skip, SSM_HEADDIM).reshape(G, 1, gw)
    per_time = pl.BlockSpec((1, R, L), lambda g, c: (g, 0, c))
    nb = SSM_INNER // SSM_STATE
    return pl.pallas_call(
        _ssd_kernel,
        out_shape=jax.ShapeDtypeStruct((t, SSM_INNER), BF16),
        grid=(G, t // L),
        in_specs=[pl.BlockSpec((L, gw), lambda g, c: (c, g)),
                  pl.BlockSpec((L, SSM_STATE), lambda g, c: (c, nb + g)),
                  pl.BlockSpec((L, SSM_STATE), lambda g, c: (c, nb + G + g)),
                  pl.BlockSpec((1, L, R), lambda g, c: (g, c, 0)), per_time, per_time,
                  pl.BlockSpec((1, 1, gw), lambda g, c: (g, 0, 0)),
                  pl.BlockSpec((L, gw), lambda g, c: (c, g)),
                  pl.BlockSpec((1, gw), lambda g, c: (0, g))],
        out_specs=pl.BlockSpec((L, gw), lambda g, c: (c, g)),
        scratch_shapes=[pltpu.VMEM((SSM_STATE, gw), F32),
                        pltpu.VMEM((R // 4, 4 * 2 * L, 256), BF16),
                        pltpu.VMEM((L, gw), F32)],
        compiler_params=_params(("parallel", "arbitrary")),
        name="ssd",
    )(xbc, xbc, xbc, acs, acst, rowp, dskip, z, norm_g.reshape(1, SSM_INNER))


def kernel(x, mix_norm, ffn_norm, final_norm, ev_w_in, ev_conv_w, ev_sgu_norm, ev_sgu_w, ev_sgu_b, ev_w_out, od_w_in, od_conv_w, od_conv_b, od_dt_bias, od_a_log, od_d, od_norm, od_w_out, peer_wq, peer_keys, peer_u, peer_v):
    h = x

    u_bf16 = peer_u.astype(BF16)
    v_t = jnp.swapaxes(peer_v, 1, 2).astype(BF16)

    def peer(xt, i):
        return peer_ffn_t(xt, i, peer_wq[i].T.astype(BF16), peer_keys[i].astype(BF16), u_bf16, v_t)

    hn = rmsnorm(h, mix_norm[0], transpose_out=False)
    proj = matmul_wcast(hn, ev_w_in, 0, ev_w_in.shape[2], bm=1024, bn=1024, name="ev_in")
    ycat = evmix(proj, ev_conv_w[0], ev_sgu_norm[0], ev_sgu_w[0], ev_sgu_b[0])
    h = matmul(ycat, ev_w_out[0].astype(BF16), bm=1024, bn=1024, res=h, name="ev_out")
    delta_t = peer(rmsnorm(h, ffn_norm[0], transpose_out=True), 0)

    h, hn = resid_norm(h, delta_t, mix_norm[1], want_h=True, norm_dtype=BF16)
    w_in = od_w_in
    z = matmul_wcast(hn, w_in, 0, SSM_INNER, bm=1024, bn=1024, name="od_in_z")
    xbc = matmul_wcast(hn, w_in, SSM_INNER, SSM_CONV_DIM, bm=1024, bn=1024, name="od_in_xbc")
    dt_raw = matmul_wcast(hn, w_in, SSM_INNER + SSM_CONV_DIM, SSM_HEADS, bm=1024, bn=SSM_HEADS,
                          name="od_in_dt")
    xbc = conv_silu(xbc, od_conv_w[0], od_conv_b[0])
    yn = ssd(xbc, dt_raw, od_dt_bias[0], od_a_log[0], od_d[0], z, od_norm[0])
    h = matmul(yn, od_w_out[0].astype(BF16), bm=1024, bn=1024, bk=2048, res=h, name="od_out")
    delta_t = peer(rmsnorm(h, ffn_norm[1], transpose_out=True), 1)

    out = resid_norm(h, delta_t, final_norm, want_h=False, norm_dtype=F32)
    return out
```
